```python
import math
import jax, jax.numpy as jnp
from jax import lax
import numpy as np

D_MODEL = 1024
BATCH = 8
SEQ = 8192
DEPTH = 1

HEAD_DIM = 64
MIX_WIDTH = D_MODEL
A_HEADS = MIX_WIDTH // (2 * HEAD_DIM)
A_KV_HEADS = max(1, A_HEADS // 4)
B_HEADS = MIX_WIDTH // HEAD_DIM - A_HEADS
A_WIDTH = A_HEADS * HEAD_DIM
A_KV_WIDTH = A_KV_HEADS * HEAD_DIM
B_WIDTH = B_HEADS * HEAD_DIM
IN_WIDTH = A_WIDTH + 2 * A_KV_WIDTH + 3 * B_WIDTH
IN_SPLITS = (A_WIDTH, A_WIDTH + A_KV_WIDTH, A_WIDTH + 2 * A_KV_WIDTH,
             A_WIDTH + 2 * A_KV_WIDTH + B_WIDTH, A_WIDTH + 2 * A_KV_WIDTH + 2 * B_WIDTH)
ROPE_THETA = 10000.0
GRID_W = 64
Q_BLOCK = 128
DILATED_PATTERNS = ((128, 1), (512, 4), (2048, 16))
SUBSEQ_BLOCK = 64
N_GROUPS = 4
EXPERTS_PER_GROUP = 8
N_EXPERTS = N_GROUPS * EXPERTS_PER_GROUP
TOP_K = 2
D_EXPERT = D_MODEL // 2
MOE_BLOCK = 512
EPS = 1e-6
NEG_INF = -1e30

kernel_name = "hymba_axial_gqa_longnet_hmoe_encoder"


def _rms_norm(x, g):
    xf = x.astype(jnp.float32)
    y = xf * lax.rsqrt(jnp.mean(xf * xf, axis=-1, keepdims=True) + EPS)
    return (y * g.astype(jnp.float32)).astype(x.dtype)


def _rope_inv_freq(dim):
    return 1.0 / (ROPE_THETA ** (jnp.arange(0, dim, 2, dtype=jnp.float32) / dim))


def _apply_rotary(x, angles):
    cos, sin = jnp.cos(angles), jnp.sin(angles)
    xf = x.astype(jnp.float32)
    x1, x2 = jnp.split(xf, 2, axis=-1)
    return jnp.concatenate([x1 * cos - x2 * sin, x2 * cos + x1 * sin], axis=-1).astype(x.dtype)


def _axial_angles(seq_len):
    rows = seq_len // GRID_W
    row = jnp.repeat(jnp.arange(rows, dtype=jnp.float32), GRID_W)
    col = jnp.tile(jnp.arange(GRID_W, dtype=jnp.float32), rows)
    f = _rope_inv_freq(HEAD_DIM // 2)
    return jnp.concatenate([row[:, None] * f, col[:, None] * f], axis=-1)


def _linear_angles(seq_len):
    t = jnp.arange(seq_len, dtype=jnp.float32)
    return t[:, None] * _rope_inv_freq(HEAD_DIM)


def _global_axial_gqa(q, k, v, q_norm_g, k_norm_g):
    b, s, _ = q.shape
    grp = A_HEADS // A_KV_HEADS
    q = q.reshape(b, s, A_KV_HEADS, grp, HEAD_DIM).transpose(0, 2, 3, 1, 4)
    k = k.reshape(b, s, A_KV_HEADS, HEAD_DIM).transpose(0, 2, 1, 3)
    v = v.reshape(b, s, A_KV_HEADS, HEAD_DIM).transpose(0, 2, 1, 3)
    ang = _axial_angles(s)
    q = _apply_rotary(_rms_norm(q, q_norm_g), ang)
    k = _apply_rotary(_rms_norm(k, k_norm_g), ang)
    nq = s // Q_BLOCK
    qb = jnp.moveaxis(q.reshape(b, A_KV_HEADS, grp, nq, Q_BLOCK, HEAD_DIM), 3, 0)
    scale = HEAD_DIM ** -0.5

    def block(qblk):
        sc = jnp.einsum('bkgqd,bksd->bkgqs', qblk, k, preferred_element_type=jnp.float32) * scale
        p = jax.nn.softmax(sc, axis=-1).astype(v.dtype)
        return jnp.einsum('bkgqs,bksd->bkgqd', p, v)

    o = lax.map(block, qb)
    o = jnp.moveaxis(o, 0, 3).reshape(b, A_KV_HEADS, grp, s, HEAD_DIM)
    return o.transpose(0, 3, 1, 2, 4).reshape(b, s, A_WIDTH)


def _dilated_window(q, k, v, window, dilation):
    b, h, s, d = q.shape
    r = window // (2 * dilation)
    L = s // dilation

    def to_sub(t):
        return t.reshape(b, h, L, dilation, d).transpose(0, 1, 3, 2, 4)

    qs, ks, vs = to_sub(q), to_sub(k), to_sub(v)
    nblk = -(-L // SUBSEQ_BLOCK)
    lp = nblk * SUBSEQ_BLOCK
    pad4 = ((0, 0), (0, 0), (0, 0))
    qs = jnp.pad(qs, pad4 + ((0, lp - L), (0, 0)))
    kpad = jnp.pad(ks, pad4 + ((r, lp - L + r), (0, 0)))
    vpad = jnp.pad(vs, pad4 + ((r, lp - L + r), (0, 0)))
    kwin = SUBSEQ_BLOCK + 2 * r
    kidx = jnp.arange(nblk)[:, None] * SUBSEQ_BLOCK + jnp.arange(kwin)[None, :]
    kb = jnp.take(kpad, kidx, axis=3)
    vb = jnp.take(vpad, kidx, axis=3)
    qb = qs.reshape(b, h, dilation, nblk, SUBSEQ_BLOCK, d)
    sc = jnp.einsum('bhrnqd,bhrnkd->bhrnqk', qb, kb, preferred_element_type=jnp.float32) * (d ** -0.5)
    qpos = jnp.arange(lp).reshape(nblk, SUBSEQ_BLOCK)
    kpos = kidx - r
    mask = (jnp.abs(qpos[:, :, None] - kpos[:, None, :]) <= r) & ((kpos >= 0) & (kpos < L))[:, None, :]
    sc = jnp.where(mask, sc, NEG_INF)
    m = jnp.max(sc, axis=-1, keepdims=True)
    p = jnp.exp(sc - m)
    den = jnp.sum(p, axis=-1, keepdims=True)
    o = jnp.einsum('bhrnqk,bhrnkd->bhrnqd', p.astype(vb.dtype), vb,
                   preferred_element_type=jnp.float32) / den
    lse = (m + jnp.log(den))[..., 0]
    o = o.astype(v.dtype).reshape(b, h, dilation, lp, d)[:, :, :, :L]
    o = o.transpose(0, 1, 3, 2, 4).reshape(b, h, s, d)
    lse = lse.reshape(b, h, dilation, lp)[:, :, :, :L].transpose(0, 1, 3, 2).reshape(b, h, s)
    return o, lse


def _longnet_mixture(q, k, v):
    b, s, _ = q.shape
    q = q.reshape(b, s, B_HEADS, HEAD_DIM).transpose(0, 2, 1, 3)
    k = k.reshape(b, s, B_HEADS, HEAD_DIM).transpose(0, 2, 1, 3)
    v = v.reshape(b, s, B_HEADS, HEAD_DIM).transpose(0, 2, 1, 3)
    ang = _linear_angles(s)
    q = _apply_rotary(q, ang)
    k = _apply_rotary(k, ang)
    outs, lses = [], []
    for window, dilation in DILATED_PATTERNS:
        o, lse = _dilated_window(q, k, v, window, dilation)
        outs.append(o)
        lses.append(lse)
    w = jax.nn.softmax(jnp.stack(lses, axis=0), axis=0).astype(v.dtype)
    o = jnp.einsum('pbhs,pbhsd->bhsd', w, jnp.stack(outs, axis=0))
    return o.transpose(0, 2, 1, 3).reshape(b, s, B_WIDTH)


def _hierarchical_moe(x, router_group_w, router_group_b, router_expert_w, router_expert_b,
                      w_gate, w_up, w_down):
    b, s, dm = x.shape
    n = b * s
    xt = x.reshape(n, dm)
    g_logits = (xt @ router_group_w).astype(jnp.float32) + router_group_b.astype(jnp.float32)
    g_prob = jax.nn.softmax(g_logits, axis=-1)
    g_top_p, g_top = lax.top_k(g_prob, 1)
    e_logits = (xt @ router_expert_w).astype(jnp.float32) + router_expert_b.astype(jnp.float32)
    e_logits = e_logits.reshape(n, N_GROUPS, EXPERTS_PER_GROUP)
    e_in_group = jnp.take_along_axis(e_logits, g_top[:, :, None], axis=1)[:, 0]
    e_top_logit, e_top = lax.top_k(e_in_group, TOP_K)
    gates = g_top_p * jax.nn.softmax(e_top_logit, axis=-1)
    expert_id = g_top * EXPERTS_PER_GROUP + e_top

    nk = n * TOP_K
    flat_e = expert_id.reshape(nk).astype(jnp.int32)
    flat_tok = jnp.arange(nk, dtype=jnp.int32) // TOP_K
    flat_gate = gates.reshape(nk)
    order = jnp.argsort(flat_e)
    se = flat_e[order]
    counts = jnp.bincount(flat_e, length=N_EXPERTS)
    start = jnp.cumsum(counts) - counts
    padded = (counts + MOE_BLOCK - 1) // MOE_BLOCK * MOE_BLOCK
    pend = jnp.cumsum(padded)
    pstart = pend - padded
    dest = pstart[se] + (jnp.arange(nk, dtype=jnp.int32) - start[se])
    n_blocks = -(-nk // MOE_BLOCK) + N_EXPERTS
    rows = n_blocks * MOE_BLOCK
    row_tok = jnp.zeros((rows,), jnp.int32).at[dest].set(flat_tok[order])
    row_gate = jnp.zeros((rows,), x.dtype).at[dest].set(flat_gate[order].astype(x.dtype))
    block_e = jnp.minimum(jnp.searchsorted(pend, jnp.arange(n_blocks) * MOE_BLOCK, side='right'),
                          N_EXPERTS - 1).astype(jnp.int32)

    def expert_block(args):
        e, tok = args
        xb = xt[tok]
        hdn = jax.nn.silu(xb @ w_gate[e]) * (xb @ w_up[e])
        return hdn @ w_down[e]

    yb = lax.map(expert_block, (block_e, row_tok.reshape(n_blocks, MOE_BLOCK)))
    y = jnp.zeros((n, dm), x.dtype).at[row_tok].add(yb.reshape(rows, dm) * row_gate[:, None])
    return y.reshape(b, s, dm)


def setup_inputs(seed: int = 0) -> dict:
    key = jax.random.key(seed)
    ks = jax.random.split(key, 20)
    f32 = jnp.float32

    def nrm(k, shape, scale):
        return jax.random.normal(k, shape, f32) * scale

    def gain(k, shape):
        return 1.0 + 0.02 * jax.random.normal(k, shape, f32)

    return {
        "x": jax.random.normal(ks[0], (BATCH, SEQ, D_MODEL), f32),
        "norm1_g": gain(ks[1], (DEPTH, D_MODEL)),
        "w_in": nrm(ks[2], (DEPTH, D_MODEL, IN_WIDTH), D_MODEL ** -0.5),
        "q_norm_g": gain(ks[3], (DEPTH, HEAD_DIM)),
        "k_norm_g": gain(ks[4], (DEPTH, HEAD_DIM)),
        "out_norm_a_g": gain(ks[5], (DEPTH, A_WIDTH)),
        "out_norm_b_g": gain(ks[6], (DEPTH, B_WIDTH)),
        "w_out": nrm(ks[7], (DEPTH, MIX_WIDTH, D_MODEL), MIX_WIDTH ** -0.5),
        "norm2_g": gain(ks[8], (DEPTH, D_MODEL)),
        "router_group_w": nrm(ks[9], (DEPTH, D_MODEL, N_GROUPS), D_MODEL ** -0.5),
        "router_group_b": nrm(ks[10], (DEPTH, N_GROUPS), 0.01),
        "router_expert_w": nrm(ks[11], (DEPTH, D_MODEL, N_EXPERTS), D_MODEL ** -0.5),
        "router_expert_b": nrm(ks[12], (DEPTH, N_EXPERTS), 0.01),
        "w_gate": nrm(ks[13], (DEPTH, N_EXPERTS, D_MODEL, D_EXPERT), D_MODEL ** -0.5),
        "w_up": nrm(ks[14], (DEPTH, N_EXPERTS, D_MODEL, D_EXPERT), D_MODEL ** -0.5),
        "w_down": nrm(ks[15], (DEPTH, N_EXPERTS, D_EXPERT, D_MODEL), D_EXPERT ** -0.5),
        "final_norm_g": gain(ks[16], (D_MODEL,)),
    }


def reference(x, norm1_g, w_in, q_norm_g, k_norm_g, out_norm_a_g, out_norm_b_g, w_out,
              norm2_g, router_group_w, router_group_b, router_expert_w, router_expert_b,
              w_gate, w_up, w_down, final_norm_g):
    h = x
    for l in range(DEPTH):
        u = _rms_norm(h, norm1_g[l])
        proj = u @ w_in[l]
        qa, ka, va, qb, kb, vb = jnp.split(proj, IN_SPLITS, axis=-1)
        oa = _rms_norm(_global_axial_gqa(qa, ka, va, q_norm_g[l], k_norm_g[l]), out_norm_a_g[l])
        ob = _rms_norm(_longnet_mixture(qb, kb, vb), out_norm_b_g[l])
        h = h + jnp.concatenate([oa, ob], axis=-1) @ w_out[l]
        h = h + _hierarchical_moe(_rms_norm(h, norm2_g[l]), router_group_w[l], router_group_b[l],
                                  router_expert_w[l], router_expert_b[l],
                                  w_gate[l], w_up[l], w_down[l])
    return _rms_norm(h, final_norm_g)
```

```python
import functools

import jax
import jax.numpy as jnp
from jax import lax
from jax.experimental import pallas as pl
from jax.experimental.pallas import tpu as pltpu

F32 = jnp.float32
BF16 = jnp.bfloat16

HEAD_DIM = 64
A_HEADS = 8
A_KV_HEADS = 2
A_GROUP = A_HEADS // A_KV_HEADS
B_HEADS = 8
A_WIDTH = A_HEADS * HEAD_DIM
A_KV_WIDTH = A_KV_HEADS * HEAD_DIM
B_WIDTH = B_HEADS * HEAD_DIM
ROPE_THETA = 10000.0
GRID_W = 64
DILATIONS = (1, 4, 16)
WINDOW_RADIUS = 64
N_GROUPS = 4
EXPERTS_PER_GROUP = 8
N_EXPERTS = N_GROUPS * EXPERTS_PER_GROUP
MOE_BLOCK = 512
EPS = 1e-6
NEG_INF = -1e30

LANES = 128
HEADS_PER_SLAB = LANES // HEAD_DIM
B_SLABS = B_WIDTH // LANES
VMEM_LIMIT_BYTES = 56 * 1024 * 1024

PROJ_ROWS = 512
ATTN_A_Q = 128
ATTN_A_K = 512
ATTN_B_ROWS = 128
ATTN_B_POS = 2048

_QA = (0, A_WIDTH)
_KA = (_QA[1], _QA[1] + 2 * A_KV_WIDTH)
_VA = (_KA[1], _KA[1] + 2 * A_KV_WIDTH)
_QB = (_VA[1], _VA[1] + B_WIDTH)
_KB = (_QB[1], _QB[1] + B_WIDTH)
_VB = (_KB[1], _KB[1] + B_WIDTH)
EXT_WIDTH = _VB[1]
QK_A_WIDTH = _KA[1]


def _params(*semantics):
    return pltpu.CompilerParams(dimension_semantics=semantics,
                                vmem_limit_bytes=VMEM_LIMIT_BYTES)


def _rms(xf, g):
    return xf * lax.rsqrt(jnp.mean(xf * xf, axis=-1, keepdims=True) + EPS) * g


def _swap_half_heads(x):
    lane = lax.broadcasted_iota(jnp.int32, x.shape, 1)
    first = (lane % HEAD_DIM) < (HEAD_DIM // 2)
    return jnp.where(first, pltpu.roll(x, LANES - HEAD_DIM // 2, 1),
                     pltpu.roll(x, HEAD_DIM // 2, 1))


def _rotary(x, cos, sin_signed):
    return x * cos + _swap_half_heads(x) * sin_signed


def _proj_kernel(x_ref, g1_ref, w_ref, ones_ref, gqk_ref, cosa_ref, sina_ref, cosb_ref, sinb_ref,
                 qa_ref, ka_ref, va_ref, qb_ref, kb_ref, vb_ref):
    u = _rms(x_ref[...], g1_ref[...]).astype(BF16)

    def proj(cols):
        return jnp.dot(u, w_ref[:, cols[0]:cols[1]], preferred_element_type=F32)

    a = proj((0, QK_A_WIDTH))
    ss = jnp.dot((a * a).astype(BF16), ones_ref[...], preferred_element_type=F32)
    an = a * lax.rsqrt(ss * (1.0 / HEAD_DIM) + EPS) * gqk_ref[...]
    cosa, sina = cosa_ref[...], sina_ref[...]
    for c in range(QK_A_WIDTH // LANES):
        r = _rotary(an[:, c * LANES:(c + 1) * LANES], cosa, sina).astype(BF16)
        if c < A_WIDTH // LANES:
            qa_ref[:, c * LANES:(c + 1) * LANES] = r
        else:
            c2 = c - A_WIDTH // LANES
            ka_ref[:, c2 * LANES:(c2 + 1) * LANES] = r
    va_ref[...] = proj(_VA).astype(BF16)

    cosb, sinb = cosb_ref[...], sinb_ref[...]
    for cols, out_ref in ((_QB, qb_ref), (_KB, kb_ref)):
        t = proj(cols)
        for c in range(B_SLABS):
            out_ref[c] = _rotary(t[:, c * LANES:(c + 1) * LANES], cosb, sinb).astype(BF16)
    t = proj(_VB)
    for c in range(B_SLABS):
        vb_ref[c] = t[:, c * LANES:(c + 1) * LANES].astype(BF16)


def _proj_call(x2, g1, w_ext, ones_bd, gqk, cosa, sina, cosb, sinb, seq):
    n, dm = x2.shape
    tm = PROJ_ROWS
    pos_blocks = seq // tm
    const = lambda i: (0, 0)
    row = lambda i: (i, 0)
    pos = lambda i: (i % pos_blocks, 0)
    slab = lambda i: (0, i, 0)
    return pl.pallas_call(
        _proj_kernel,
        grid=(n // tm,),
        in_specs=[
            pl.BlockSpec((tm, dm), row),
            pl.BlockSpec((1, dm), const),
            pl.BlockSpec((dm, EXT_WIDTH), const),
            pl.BlockSpec((QK_A_WIDTH, QK_A_WIDTH), const),
            pl.BlockSpec((1, QK_A_WIDTH), const),
            pl.BlockSpec((tm, LANES), pos),
            pl.BlockSpec((tm, LANES), pos),
            pl.BlockSpec((tm, LANES), pos),
            pl.BlockSpec((tm, LANES), pos),
        ],
        out_specs=[
            pl.BlockSpec((tm, A_WIDTH), row),
            pl.BlockSpec((tm, 2 * A_KV_WIDTH), row),
            pl.BlockSpec((tm, 2 * A_KV_WIDTH), row),
            pl.BlockSpec((B_SLABS, tm, LANES), slab),
            pl.BlockSpec((B_SLABS, tm, LANES), slab),
            pl.BlockSpec((B_SLABS, tm, LANES), slab),
        ],
        out_shape=[
            jax.ShapeDtypeStruct((n, A_WIDTH), BF16),
            jax.ShapeDtypeStruct((n, 2 * A_KV_WIDTH), BF16),
            jax.ShapeDtypeStruct((n, 2 * A_KV_WIDTH), BF16),
            jax.ShapeDtypeStruct((B_SLABS, n, LANES), BF16),
            jax.ShapeDtypeStruct((B_SLABS, n, LANES), BF16),
            jax.ShapeDtypeStruct((B_SLABS, n, LANES), BF16),
        ],
        compiler_params=_params("parallel"),
        name="proj",
    )(x2, g1, w_ext, ones_bd, gqk, cosa, sina, cosb, sinb)


def _attn_a_kernel(q_ref, k_ref, v_ref, o_ref, qs_ref, m_ref, l_ref, acc_ref, *, tq, tk, nk):
    lane = lax.broadcasted_iota(jnp.int32, (tq, LANES), 1)
    upper = lane >= HEAD_DIM
    for g in range(A_GROUP):
        slab = q_ref[0, :, (g // 2) * LANES:(g // 2 + 1) * LANES]
        keep = upper if g % 2 else jnp.logical_not(upper)
        qs_ref[g * tq:(g + 1) * tq, :] = jnp.where(keep, slab, jnp.zeros_like(slab))
    m_ref[...] = jnp.full(m_ref.shape, NEG_INF, F32)
    l_ref[...] = jnp.zeros(l_ref.shape, F32)
    acc_ref[...] = jnp.zeros(acc_ref.shape, F32)

    def body(kc, carry):
        start = pl.multiple_of(kc * tk, tk)
        k = k_ref[0, pl.ds(start, tk), :]
        v = v_ref[0, pl.ds(start, tk), :]
        s = lax.dot_general(qs_ref[...], k, (((1,), (1,)), ((), ())),
                            preferred_element_type=F32)
        m_prev = m_ref[...]
        m_new = jnp.maximum(m_prev, jnp.max(s, axis=-1, keepdims=True))
        alpha = jnp.exp(m_prev - m_new)
        p = jnp.exp(s - m_new)
        l_ref[...] = alpha * l_ref[...] + jnp.sum(p, axis=-1, keepdims=True)
        acc_ref[...] = alpha * acc_ref[...] + jnp.dot(p.astype(BF16), v,
                                                      preferred_element_type=F32)
        m_ref[...] = m_new
        return carry

    lax.fori_loop(0, nk, body, 0)
    o = acc_ref[...] / l_ref[...]
    for j in range(A_GROUP // 2):
        lo = o[(2 * j) * tq:(2 * j + 1) * tq]
        hi = o[(2 * j + 1) * tq:(2 * j + 2) * tq]
        o_ref[0, :, j * LANES:(j + 1) * LANES] = jnp.where(upper, hi, lo).astype(o_ref.dtype)


def _attn_a_call(qa, ka, va):
    b, s, _ = qa.shape
    tq, tk = ATTN_A_Q, ATTN_A_K
    group_w = A_GROUP * HEAD_DIM
    kern = functools.partial(_attn_a_kernel, tq=tq, tk=tk, nk=s // tk)
    return pl.pallas_call(
        kern,
        grid=(b, A_KV_HEADS, s // tq),
        in_specs=[
            pl.BlockSpec((1, tq, group_w), lambda bi, kv, qi: (bi, qi, kv)),
            pl.BlockSpec((1, s, LANES), lambda bi, kv, qi: (bi, 0, kv)),
            pl.BlockSpec((1, s, LANES), lambda bi, kv, qi: (bi, 0, kv)),
        ],
        out_specs=pl.BlockSpec((1, tq, group_w), lambda bi, kv, qi: (bi, qi, kv)),
        out_shape=jax.ShapeDtypeStruct((b, s, A_WIDTH), BF16),
        scratch_shapes=[
            pltpu.VMEM((A_GROUP * tq, LANES), BF16),
            pltpu.VMEM((A_GROUP * tq, 1), F32),
            pltpu.VMEM((A_GROUP * tq, 1), F32),
            pltpu.VMEM((A_GROUP * tq, LANES), F32),
        ],
        compiler_params=_params("parallel", "parallel", "parallel"),
        name="attn_a",
    )(qa, ka, va)


def _attn_b_kernel(q_ref, k_ref, v_ref, o_ref, lse_ref, *, dil, sub_len, rows_per_step):
    rb = ATTN_B_ROWS
    kw = rb + 2 * WINDOW_RADIUS
    step = pl.program_id(2)
    lane = lax.broadcasted_iota(jnp.int32, (rb, LANES), 1)
    upper = lane >= HEAD_DIM
    rel0 = (lax.broadcasted_iota(jnp.int32, (rb, kw), 0)
            - lax.broadcasted_iota(jnp.int32, (rb, kw), 1))
    for rt in range(rows_per_step // rb):
        j0 = step * rows_per_step + rt * rb
        ws = jnp.clip(j0 - WINDOW_RADIUS, 0, sub_len - kw)
        ws = pl.multiple_of(ws, WINDOW_RADIUS)
        rel = rel0 + (j0 - ws)
        valid = jnp.abs(rel) <= WINDOW_RADIUS
        for off in range(dil):
            cols = slice(off * LANES, (off + 1) * LANES)
            q = q_ref[0, 0, rt * rb:(rt + 1) * rb, cols]
            kwin = k_ref[0, 0, pl.ds(ws, kw), cols]
            vwin = v_ref[0, 0, pl.ds(ws, kw), cols]
            o_slab = None
            lse_slab = None
            for h in range(HEADS_PER_SLAB):
                keep = upper if h else jnp.logical_not(upper)
                qm = jnp.where(keep, q, jnp.zeros_like(q))
                s = lax.dot_general(qm, kwin, (((1,), (1,)), ((), ())),
                                    preferred_element_type=F32)
                s = jnp.where(valid, s, NEG_INF)
                m = jnp.max(s, axis=-1, keepdims=True)
                p = jnp.exp(s - m)
                den = jnp.sum(p, axis=-1, keepdims=True)
                o = jnp.dot(p.astype(BF16), vwin, preferred_element_type=F32) / den
                lse = jnp.broadcast_to(m + jnp.log(den), (rb, LANES))
                o_slab = o if h == 0 else jnp.where(upper, o, o_slab)
                lse_slab = lse if h == 0 else jnp.where(upper, lse, lse_slab)
            o_ref[0, 0, rt * rb:(rt + 1) * rb, cols] = o_slab.astype(o_ref.dtype)
            lse_ref[0, 0, rt * rb:(rt + 1) * rb, cols] = lse_slab


def _attn_b_call(q4, k4, v4, dil):
    nslab, b, s, _ = q4.shape
    sub_len = s // dil
    rows = ATTN_B_POS // dil
    view = (nslab, b, sub_len, dil * LANES)
    kern = functools.partial(_attn_b_kernel, dil=dil, sub_len=sub_len, rows_per_step=rows)
    tile = pl.BlockSpec((1, 1, rows, dil * LANES), lambda sl, bi, t: (sl, bi, t, 0))
    full = pl.BlockSpec((1, 1, sub_len, dil * LANES), lambda sl, bi, t: (sl, bi, 0, 0))
    o, lse = pl.pallas_call(
        kern,
        grid=(nslab, b, sub_len // rows),
        in_specs=[tile, full, full],
        out_specs=[tile, tile],
        out_shape=[jax.ShapeDtypeStruct(view, BF16), jax.ShapeDtypeStruct(view, F32)],
        compiler_params=_params("parallel", "parallel", "parallel"),
        name=f"attn_b_d{dil}",
    )(q4.reshape(view), k4.reshape(view), v4.reshape(view))
    return o.reshape(nslab, b * s, LANES), lse.reshape(nslab, b * s, LANES)


def _out_route_kernel(x_ref, oa_ref, o1_ref, o2_ref, o3_ref, l1_ref, l2_ref, l3_ref,
                      ga_ref, gb_ref, wout_ref, g2_ref, whi_ref, wlo_ref, rb_ref, tri_ref,
                      h_ref, route_ref, counts_ref, base_ref):
    tm = x_ref.shape[0]

    @pl.when(pl.program_id(0) == 0)
    def _():
        base_ref[...] = jnp.zeros(base_ref.shape, F32)

    oa = _rms(oa_ref[...].astype(F32), ga_ref[...])
    slabs = []
    for c in range(B_SLABS):
        l1, l2, l3 = l1_ref[c], l2_ref[c], l3_ref[c]
        lmax = jnp.maximum(jnp.maximum(l1, l2), l3)
        e1, e2, e3 = jnp.exp(l1 - lmax), jnp.exp(l2 - lmax), jnp.exp(l3 - lmax)
        mix = (e1 * o1_ref[c].astype(F32) + e2 * o2_ref[c].astype(F32)
               + e3 * o3_ref[c].astype(F32)) / (e1 + e2 + e3)
        slabs.append(mix)
    ob = _rms(jnp.concatenate(slabs, axis=-1), gb_ref[...])
    cat = jnp.concatenate([oa, ob], axis=-1).astype(BF16)
    h = x_ref[...] + jnp.dot(cat, wout_ref[...], preferred_element_type=F32)
    h_ref[...] = h

    xt = _rms(h, g2_ref[...])
    xh = xt.astype(BF16)
    xl = (xt - xh.astype(F32)).astype(BF16)
    whi = whi_ref[...]
    logits = (jnp.dot(xh, whi, preferred_element_type=F32)
              + jnp.dot(xl, whi, preferred_element_type=F32)
              + jnp.dot(xh, wlo_ref[...], preferred_element_type=F32)) + rb_ref[...]

    lane = lax.broadcasted_iota(jnp.int32, (tm, LANES), 1).astype(F32)
    none = float(LANES)

    def first_argmax(vals):
        top = jnp.max(vals, axis=-1, keepdims=True)
        idx = jnp.min(jnp.where(vals == top, lane, none), axis=-1, keepdims=True)
        return top, idx

    gl = jnp.where(lane < N_GROUPS, logits, -jnp.inf)
    gmax, gidx = first_argmax(gl)
    gprob = 1.0 / jnp.sum(jnp.exp(gl - gmax), axis=-1, keepdims=True)
    lo = N_GROUPS + EXPERTS_PER_GROUP * gidx
    el = jnp.where((lane >= lo) & (lane < lo + EXPERTS_PER_GROUP), logits, -jnp.inf)
    v1, i1 = first_argmax(el)
    v2, i2 = first_argmax(jnp.where(lane == i1, -jnp.inf, el))
    t = jnp.exp(v2 - v1)
    gate1 = gprob / (1.0 + t)
    gate2 = gprob * t / (1.0 + t)
    x1 = i1 - N_GROUPS
    x2 = i2 - N_GROUPS

    hot1 = lane == x1
    hot2 = lane == x2
    hot = jnp.where(hot1 | hot2, 1.0, 0.0)
    before = base_ref[...] + jnp.dot(tri_ref[...], hot.astype(BF16), preferred_element_type=F32)
    r1 = jnp.sum(jnp.where(hot1, before, 0.0), axis=-1, keepdims=True)
    r2 = jnp.sum(jnp.where(hot2, before, 0.0), axis=-1, keepdims=True)
    base = base_ref[...] + jnp.sum(hot, axis=0, keepdims=True)
    base_ref[...] = base
    counts_ref[...] = jnp.broadcast_to(base, counts_ref.shape)

    packed = jnp.zeros((tm, LANES), F32)
    for i, col in enumerate((x1, x2, gate1, gate2, r1, r2)):
        packed = jnp.where(lane == i, col, packed)
    route_ref[...] = packed


def _out_route_call(x2, oa, o_pats, lse_pats, ga, gb, wout, g2, whi, wlo, rbias, tri):
    n, dm = x2.shape
    tm = PROJ_ROWS
    const = lambda i: (0, 0)
    row = lambda i: (i, 0)
    slab = pl.BlockSpec((B_SLABS, tm, LANES), lambda i: (0, i, 0))
    return pl.pallas_call(
        _out_route_kernel,
        grid=(n // tm,),
        in_specs=[
            pl.BlockSpec((tm, dm), row),
            pl.BlockSpec((tm, A_WIDTH), row),
            slab, slab, slab, slab, slab, slab,
            pl.BlockSpec((1, A_WIDTH), const),
            pl.BlockSpec((1, B_WIDTH), const),
            pl.BlockSpec((A_WIDTH + B_WIDTH, dm), const),
            pl.BlockSpec((1, dm), const),
            pl.BlockSpec((dm, LANES), const),
            pl.BlockSpec((dm, LANES), const),
            pl.BlockSpec((1, LANES), const),
            pl.BlockSpec((tm, tm), const),
        ],
        out_specs=[
            pl.BlockSpec((tm, dm), row),
            pl.BlockSpec((tm, LANES), row),
            pl.BlockSpec((8, LANES), const),
        ],
        out_shape=[
            jax.ShapeDtypeStruct((n, dm), F32),
            jax.ShapeDtypeStruct((n, LANES), F32),
            jax.ShapeDtypeStruct((8, LANES), F32),
        ],
        scratch_shapes=[pltpu.VMEM((1, LANES), F32)],
        compiler_params=_params("arbitrary"),
        name="out_route",
    )(x2, oa, *o_pats, *lse_pats, ga, gb, wout, g2, whi, wlo, rbias, tri)


def _row_copy(src_ref, src_row, dst_ref, dst_row, sem):
    return pltpu.make_async_copy(src_ref.at[pl.ds(src_row, 1)], dst_ref.at[pl.ds(dst_row, 1)], sem)


def _push_kernel(dest_ref, h_ref, g2_ref, xs_in_ref, xs_ref, xt_ref, sem):
    del xs_in_ref
    tm = h_ref.shape[0]
    xt_ref[...] = _rms(h_ref[...], g2_ref[...])

    def start(i, carry):
        for c in range(2):
            _row_copy(xt_ref, i, xs_ref, dest_ref[0, 0, 2 * i + c], sem).start()
        return carry

    lax.fori_loop(0, tm, start, 0)

    def wait(i, carry):
        _row_copy(xt_ref, 0, xs_ref, 0, sem).wait()
        return carry

    lax.fori_loop(0, 2 * tm, wait, 0)


def _push_call(dest3, h, g2, xs_zero):
    n, dm = h.shape
    tm = PROJ_ROWS
    return pl.pallas_call(
        _push_kernel,
        grid=(n // tm,),
        in_specs=[
            pl.BlockSpec((1, 1, 2 * tm), lambda i: (i, 0, 0), memory_space=pltpu.SMEM),
            pl.BlockSpec((tm, dm), lambda i: (i, 0)),
            pl.BlockSpec((1, dm), lambda i: (0, 0)),
            pl.BlockSpec(memory_space=pl.ANY),
        ],
        out_specs=pl.BlockSpec(memory_space=pl.ANY),
        out_shape=jax.ShapeDtypeStruct(xs_zero.shape, F32),
        scratch_shapes=[pltpu.VMEM((tm, dm), F32), pltpu.SemaphoreType.DMA(())],
        input_output_aliases={3: 0},
        compiler_params=_params("arbitrary"),
        name="push",
    )(dest3, h, g2, xs_zero)


def _moe_kernel(be_ref, nv_ref, xs_ref, wg_ref, wu_ref, wd_ref, y_ref):
    del be_ref

    @pl.when(pl.program_id(0) < nv_ref[0])
    def _():
        xb = xs_ref[...].astype(BF16)
        a = jnp.dot(xb, wg_ref[0], preferred_element_type=F32)
        u = jnp.dot(xb, wu_ref[0], preferred_element_type=F32)
        hdn = (a / (1.0 + jnp.exp(-a))) * u
        y_ref[...] = jnp.dot(hdn.astype(BF16), wd_ref[0], preferred_element_type=F32)


def _moe_call(block_e, n_valid, xs, wg, wu, wd):
    rows, dm = xs.shape
    de = wg.shape[-1]
    blk = lambda i, be, nv: (jnp.minimum(i, nv[0] - 1), 0)
    wsel = lambda i, be, nv: (be[jnp.minimum(i, nv[0] - 1)], 0, 0)
    grid_spec = pltpu.PrefetchScalarGridSpec(
        num_scalar_prefetch=2,
        grid=(rows // MOE_BLOCK,),
        in_specs=[
            pl.BlockSpec((MOE_BLOCK, dm), blk),
            pl.BlockSpec((1, dm, de), wsel),
            pl.BlockSpec((1, dm, de), wsel),
            pl.BlockSpec((1, de, dm), wsel),
        ],
        out_specs=pl.BlockSpec((MOE_BLOCK, dm), blk),
    )
    return pl.pallas_call(
        _moe_kernel,
        grid_spec=grid_spec,
        out_shape=jax.ShapeDtypeStruct((rows, dm), F32),
        compiler_params=_params("arbitrary"),
        name="moe",
    )(block_e, n_valid, xs, wg, wu, wd)


def _final_kernel(dest_ref, h_ref, route_ref, gf_ref, yb_ref, out_ref, y1_ref, y2_ref, sem):
    tm = h_ref.shape[0]

    def start(i, carry):
        _row_copy(yb_ref, dest_ref[0, 0, 2 * i], y1_ref, i, sem).start()
        _row_copy(yb_ref, dest_ref[0, 0, 2 * i + 1], y2_ref, i, sem).start()
        return carry

    lax.fori_loop(0, tm, start, 0)

    def wait(i, carry):
        _row_copy(yb_ref, 0, y1_ref, 0, sem).wait()
        return carry

    lax.fori_loop(0, 2 * tm, wait, 0)
    route = route_ref[...]
    y = route[:, 2:3] * y1_ref[...] + route[:, 3:4] * y2_ref[...]
    out_ref[...] = _rms(h_ref[...] + y, gf_ref[...])


def _final_call(dest3, h, route, gf, yb):
    n, dm = h.shape
    tm = PROJ_ROWS
    return pl.pallas_call(
        _final_kernel,
        grid=(n // tm,),
        in_specs=[
            pl.BlockSpec((1, 1, 2 * tm), lambda i: (i, 0, 0), memory_space=pltpu.SMEM),
            pl.BlockSpec((tm, dm), lambda i: (i, 0)),
            pl.BlockSpec((tm, LANES), lambda i: (i, 0)),
            pl.BlockSpec((1, dm), lambda i: (0, 0)),
            pl.BlockSpec(memory_space=pl.ANY),
        ],
        out_specs=pl.BlockSpec((tm, dm), lambda i: (i, 0)),
        out_shape=jax.ShapeDtypeStruct((n, dm), F32),
        scratch_shapes=[pltpu.VMEM((tm, dm), F32), pltpu.VMEM((tm, dm), F32),
                        pltpu.SemaphoreType.DMA(())],
        compiler_params=_params("arbitrary"),
        name="final",
    )(dest3, h, route, gf, yb)


def _rope_inv_freq(dim):
    return 1.0 / (ROPE_THETA ** (jnp.arange(0, dim, 2, dtype=F32) / dim))


def _rotary_tables(angles):
    cos = jnp.tile(jnp.cos(angles), (1, LANES // angles.shape[1]))
    sin = jnp.sin(angles)
    sin = jnp.tile(jnp.concatenate([-sin, sin], axis=-1), (1, HEADS_PER_SLAB))
    return cos, sin


def _axial_angles(seq):
    rows = seq // GRID_W
    row = jnp.repeat(jnp.arange(rows, dtype=F32), GRID_W)
    col = jnp.tile(jnp.arange(GRID_W, dtype=F32), rows)
    f = _rope_inv_freq(HEAD_DIM // 2)
    return jnp.concatenate([row[:, None] * f, col[:, None] * f], axis=-1)


def _linear_angles(seq):
    return jnp.arange(seq, dtype=F32)[:, None] * _rope_inv_freq(HEAD_DIM)


def _extended_w_in(w):
    scale = HEAD_DIM ** -0.5
    o = 0
    qa = w[:, o:o + A_WIDTH]; o += A_WIDTH
    ka = w[:, o:o + A_KV_WIDTH]; o += A_KV_WIDTH
    va = w[:, o:o + A_KV_WIDTH]; o += A_KV_WIDTH
    qb = w[:, o:o + B_WIDTH]; o += B_WIDTH
    kb = w[:, o:o + B_WIDTH]; o += B_WIDTH
    vb = w[:, o:o + B_WIDTH]

    def dup(t):
        heads = [t[:, h * HEAD_DIM:(h + 1) * HEAD_DIM] for h in range(A_KV_HEADS)]
        return jnp.concatenate([p for h in heads for p in (h, h)], axis=1)

    return jnp.concatenate([qa, dup(ka), dup(va), qb * scale, kb, vb], axis=1).astype(BF16)


def _layer(h2, batch, seq, p):
    n, dm = h2.shape
    scale = HEAD_DIM ** -0.5
    gqk = jnp.concatenate([jnp.tile(p["q_norm_g"] * scale, A_HEADS),
                           jnp.tile(p["k_norm_g"], 2 * A_KV_HEADS)])[None, :]
    gi = jnp.arange(QK_A_WIDTH) // HEAD_DIM
    ones_bd = (gi[:, None] == gi[None, :]).astype(BF16)
    cosa, sina = _rotary_tables(_axial_angles(seq))
    cosb, sinb = _rotary_tables(_linear_angles(seq))

    qa, ka, va, qb, kb, vb = _proj_call(h2, p["norm1_g"][None, :], _extended_w_in(p["w_in"]),
                                        ones_bd, gqk, cosa, sina, cosb, sinb, seq)

    oa = _attn_a_call(qa.reshape(batch, seq, A_WIDTH), ka.reshape(batch, seq, 2 * A_KV_WIDTH),
                      va.reshape(batch, seq, 2 * A_KV_WIDTH)).reshape(n, A_WIDTH)

    shape4 = (B_SLABS, batch, seq, LANES)
    q4, k4, v4 = qb.reshape(shape4), kb.reshape(shape4), vb.reshape(shape4)
    pats = [_attn_b_call(q4, k4, v4, d) for d in DILATIONS]

    wr = jnp.concatenate([p["router_group_w"], p["router_expert_w"]], axis=1)
    wr = jnp.pad(wr, ((0, 0), (0, LANES - wr.shape[1])))
    whi = wr.astype(BF16)
    wlo = (wr - whi.astype(F32)).astype(BF16)
    rbias = jnp.concatenate([p["router_group_b"], p["router_expert_b"]])
    rbias = jnp.pad(rbias, (0, LANES - rbias.shape[0]))[None, :]
    ti = jnp.arange(PROJ_ROWS)
    tri = (ti[:, None] > ti[None, :]).astype(BF16)

    h1, route, counts = _out_route_call(
        h2, oa, [o for o, _ in pats], [l for _, l in pats],
        p["out_norm_a_g"][None, :], p["out_norm_b_g"][None, :], p["w_out"].astype(BF16),
        p["norm2_g"][None, :], whi, wlo, rbias, tri)

    counts = counts[0, :N_EXPERTS].astype(jnp.int32)
    padded = (counts + MOE_BLOCK - 1) // MOE_BLOCK * MOE_BLOCK
    pend = jnp.cumsum(padded)
    pstart = pend - padded
    expert = route[:, 0:2].astype(jnp.int32)
    dest = pstart[expert] + route[:, 4:6].astype(jnp.int32)
    dest3 = dest.reshape(n // PROJ_ROWS, 1, 2 * PROJ_ROWS)
    n_blocks = -(-(2 * n) // MOE_BLOCK) + N_EXPERTS
    block_e = jnp.minimum(jnp.searchsorted(pend, jnp.arange(n_blocks) * MOE_BLOCK, side="right"),
                          N_EXPERTS - 1).astype(jnp.int32)
    n_valid = (pend[-1:] // MOE_BLOCK).astype(jnp.int32)

    xs = _push_call(dest3, h1, p["norm2_g"][None, :],
                    jnp.zeros((n_blocks * MOE_BLOCK, dm), F32))
    yb = _moe_call(block_e, n_valid, xs, p["w_gate"].astype(BF16), p["w_up"].astype(BF16),
                   p["w_down"].astype(BF16))
    return dest3, h1, route, yb


def kernel(x, norm1_g, w_in, q_norm_g, k_norm_g, out_norm_a_g, out_norm_b_g, w_out, norm2_g,
           router_group_w, router_group_b, router_expert_w, router_expert_b, w_gate, w_up, w_down,
           final_norm_g):
    batch, seq, dm = x.shape
    assert dm == A_WIDTH + B_WIDTH and norm1_g.shape[0] == 1
    assert seq % max(ATTN_B_POS, PROJ_ROWS, ATTN_A_K) == 0
    assert seq // max(DILATIONS) >= ATTN_B_ROWS + 2 * WINDOW_RADIUS
    layer = dict(norm1_g=norm1_g[0], w_in=w_in[0], q_norm_g=q_norm_g[0], k_norm_g=k_norm_g[0],
                 out_norm_a_g=out_norm_a_g[0], out_norm_b_g=out_norm_b_g[0], w_out=w_out[0],
                 norm2_g=norm2_g[0], router_group_w=router_group_w[0],
                 router_group_b=router_group_b[0], router_expert_w=router_expert_w[0],
                 router_expert_b=router_expert_b[0], w_gate=w_gate[0], w_up=w_up[0],
                 w_down=w_down[0])
    dest3, h1, route, yb = _layer(x.reshape(batch * seq, dm), batch, seq, layer)
    out = _final_call(dest3, h1, route, final_norm_g[None, :], yb)
    return out.reshape(batch, seq, dm)
```

```python
import functools

import jax
import jax.numpy as jnp
from jax import lax
from jax.experimental import pallas as pl
from jax.experimental.pallas import tpu as pltpu

F32 = jnp.float32
BF16 = jnp.bfloat16

HEAD_DIM = 64
A_HEADS = 8
A_KV_HEADS = 2
A_GROUP = A_HEADS // A_KV_HEADS
B_HEADS = 8
A_WIDTH = A_HEADS * HEAD_DIM
A_KV_WIDTH = A_KV_HEADS * HEAD_DIM
B_WIDTH = B_HEADS * HEAD_DIM
ROPE_THETA = 10000.0
GRID_W = 64
DILATIONS = (1, 4, 16)
WINDOW_RADIUS = 64
N_GROUPS = 4
EXPERTS_PER_GROUP = 8
N_EXPERTS = N_GROUPS * EXPERTS_PER_GROUP
MOE_BLOCK = 512
EPS = 1e-6
NEG_INF = -1e30
LOG2_E = 1.4426950408889634

LANES = 128
HEADS_PER_SLAB = LANES // HEAD_DIM
B_SLABS = B_WIDTH // LANES
VT_ROWS = HEAD_DIM + 16
VMEM_LIMIT_BYTES = 56 * 1024 * 1024

PROJ_ROWS = 512
ATTN_A_Q = 128
ATTN_A_K = 512
ATTN_B_ROWS = 128
ATTN_B_POS = 2048

_QA = (0, A_WIDTH)
_KA = (_QA[1], _QA[1] + 2 * A_KV_WIDTH)
_VA = (_KA[1], _KA[1] + A_KV_WIDTH)
_QB = (_VA[1], _VA[1] + B_WIDTH)
_KB = (_QB[1], _QB[1] + B_WIDTH)
_VB = (_KB[1], _KB[1] + B_WIDTH)
EXT_WIDTH = _VB[1]
QK_A_WIDTH = _KA[1]


def _params(*semantics):
    return pltpu.CompilerParams(dimension_semantics=semantics,
                                vmem_limit_bytes=VMEM_LIMIT_BYTES)


def _rms(xf, g):
    return xf * lax.rsqrt(jnp.mean(xf * xf, axis=-1, keepdims=True) + EPS) * g


def _swap_half_heads(x):
    lane = lax.broadcasted_iota(jnp.int32, x.shape, 1)
    first = (lane % HEAD_DIM) < (HEAD_DIM // 2)
    return jnp.where(first, pltpu.roll(x, LANES - HEAD_DIM // 2, 1),
                     pltpu.roll(x, HEAD_DIM // 2, 1))


def _rotary(x, cos, sin_signed):
    return x * cos + _swap_half_heads(x) * sin_signed


def _proj_kernel(x_ref, g1_ref, w_ref, ones_ref, gqk_ref, cosa_ref, sina_ref, cosb_ref, sinb_ref,
                 qat_ref, ka_ref, vat_ref, qb_ref, kb_ref, vb_ref):
    tm = x_ref.shape[0]
    u = _rms(x_ref[...], g1_ref[...]).astype(BF16)

    def proj(cols):
        return jnp.dot(u, w_ref[:, cols[0]:cols[1]], preferred_element_type=F32)

    a = proj((0, QK_A_WIDTH))
    ss = jnp.dot((a * a).astype(BF16), ones_ref[...], preferred_element_type=F32)
    an = a * lax.rsqrt(ss * (1.0 / HEAD_DIM) + EPS) * gqk_ref[...]
    cosa, sina = cosa_ref[...], sina_ref[...]
    for c in range(QK_A_WIDTH // LANES):
        r = _rotary(an[:, c * LANES:(c + 1) * LANES], cosa, sina)
        if c < A_WIDTH // LANES:
            qat_ref[0, c * LANES:(c + 1) * LANES, :] = r.T.astype(BF16)
        else:
            c2 = c - A_WIDTH // LANES
            ka_ref[:, c2 * LANES:(c2 + 1) * LANES] = r.astype(BF16)
    vat = proj(_VA).T
    row = lax.broadcasted_iota(jnp.int32, (VT_ROWS - HEAD_DIM, tm), 0)
    ones_row = jnp.where(row == 0, 1.0, 0.0).astype(BF16)
    for kv in range(A_KV_HEADS):
        vat_ref[0, kv, 0:HEAD_DIM, :] = vat[kv * HEAD_DIM:(kv + 1) * HEAD_DIM].astype(BF16)
        vat_ref[0, kv, HEAD_DIM:VT_ROWS, :] = ones_row

    cosb, sinb = cosb_ref[...], sinb_ref[...]
    for cols, out_ref in ((_QB, qb_ref), (_KB, kb_ref)):
        t = proj(cols)
        for c in range(B_SLABS):
            out_ref[c] = _rotary(t[:, c * LANES:(c + 1) * LANES], cosb, sinb).astype(BF16)
    t = proj(_VB)
    for c in range(B_SLABS):
        vb_ref[c] = t[:, c * LANES:(c + 1) * LANES].astype(BF16)


def _proj_call(x2, g1, w_ext, ones_bd, gqk, cosa, sina, cosb, sinb, seq):
    n, dm = x2.shape
    tm = PROJ_ROWS
    pos_blocks = seq // tm
    const = lambda i: (0, 0)
    row = lambda i: (i, 0)
    pos = lambda i: (i % pos_blocks, 0)
    slab = lambda i: (0, i, 0)
    batch = n // seq
    return pl.pallas_call(
        _proj_kernel,
        grid=(n // tm,),
        in_specs=[
            pl.BlockSpec((tm, dm), row),
            pl.BlockSpec((1, dm), const),
            pl.BlockSpec((dm, EXT_WIDTH), const),
            pl.BlockSpec((QK_A_WIDTH, QK_A_WIDTH), const),
            pl.BlockSpec((1, QK_A_WIDTH), const),
            pl.BlockSpec((tm, LANES), pos),
            pl.BlockSpec((tm, LANES), pos),
            pl.BlockSpec((tm, LANES), pos),
            pl.BlockSpec((tm, LANES), pos),
        ],
        out_specs=[
            pl.BlockSpec((1, A_WIDTH, tm), lambda i: (i // pos_blocks, 0, i % pos_blocks)),
            pl.BlockSpec((tm, 2 * A_KV_WIDTH), row),
            pl.BlockSpec((1, A_KV_HEADS, VT_ROWS, tm),
                         lambda i: (i // pos_blocks, 0, 0, i % pos_blocks)),
            pl.BlockSpec((B_SLABS, tm, LANES), slab),
            pl.BlockSpec((B_SLABS, tm, LANES), slab),
            pl.BlockSpec((B_SLABS, tm, LANES), slab),
        ],
        out_shape=[
            jax.ShapeDtypeStruct((batch, A_WIDTH, seq), BF16),
            jax.ShapeDtypeStruct((n, 2 * A_KV_WIDTH), BF16),
            jax.ShapeDtypeStruct((batch, A_KV_HEADS, VT_ROWS, seq), BF16),
            jax.ShapeDtypeStruct((B_SLABS, n, LANES), BF16),
            jax.ShapeDtypeStruct((B_SLABS, n, LANES), BF16),
            jax.ShapeDtypeStruct((B_SLABS, n, LANES), BF16),
        ],
        compiler_params=_params("parallel"),
        name="proj",
    )(x2, g1, w_ext, ones_bd, gqk, cosa, sina, cosb, sinb)


def _attn_a_kernel(qt_ref, k_ref, vt_ref, o_ref, w_ref, m_ref, acc_ref, s_ref, *, tq, tk, nk):
    nq = A_GROUP * tq
    w_ref[HEAD_DIM:, :] = jnp.zeros((LANES - HEAD_DIM, nq), BF16)
    for g in range(A_GROUP):
        w_ref[0:HEAD_DIM, g * tq:(g + 1) * tq] = qt_ref[0, g * HEAD_DIM:(g + 1) * HEAD_DIM, :]
    m_ref[...] = jnp.full(m_ref.shape, NEG_INF, F32)
    acc_ref[...] = jnp.zeros(acc_ref.shape, F32)

    def chunk_start(c):
        return c * tk if isinstance(c, int) else pl.multiple_of(c * tk, tk)

    def scores(c):
        k = k_ref[0, pl.ds(chunk_start(c), tk), :]
        return jnp.dot(k, w_ref[...], preferred_element_type=F32)

    def accumulate(s, c):
        vt = vt_ref[0, 0, :, pl.ds(chunk_start(c), tk)]
        s3 = s.reshape(tk // 8, 8, nq)
        m_prev = m_ref[...]
        m_new = jnp.maximum(m_prev, jnp.max(jnp.max(s3, axis=0), axis=0, keepdims=True))
        p = jnp.exp2(s3 - m_new[None]).reshape(tk, nq).astype(BF16)
        acc = acc_ref[...].reshape(VT_ROWS // 8, 8, nq) * jnp.exp2(m_prev - m_new)[None]
        acc_ref[...] = acc.reshape(VT_ROWS, nq) + jnp.dot(vt, p, preferred_element_type=F32)
        m_ref[...] = m_new

    def pair(c0, last):
        s1 = scores(c0 + 1)
        accumulate(s_ref[...], c0)
        if not last:
            s_ref[...] = scores(c0 + 2)
        accumulate(s1, c0 + 1)

    s_ref[...] = scores(0)

    def body(i, carry):
        pair(2 * i, False)
        return carry

    lax.fori_loop(0, nk // 2 - 1, body, 0)
    pair(nk - 2, True)
    acc = acc_ref[...]
    ot = acc[0:HEAD_DIM] / acc[HEAD_DIM:HEAD_DIM + 1]
    for j in range(A_GROUP // 2):
        pair = jnp.concatenate([ot[:, (2 * j) * tq:(2 * j + 1) * tq],
                                ot[:, (2 * j + 1) * tq:(2 * j + 2) * tq]], axis=0)
        o_ref[0, :, j * LANES:(j + 1) * LANES] = pair.T.astype(o_ref.dtype)


def _attn_a_call(qat, ka, vat):
    b, _, s = qat.shape
    tq, tk = ATTN_A_Q, ATTN_A_K
    group_w = A_GROUP * HEAD_DIM
    kern = functools.partial(_attn_a_kernel, tq=tq, tk=tk, nk=s // tk)
    return pl.pallas_call(
        kern,
        grid=(b, A_KV_HEADS, s // tq),
        in_specs=[
            pl.BlockSpec((1, group_w, tq), lambda bi, kv, qi: (bi, kv, qi)),
            pl.BlockSpec((1, s, LANES), lambda bi, kv, qi: (bi, 0, kv)),
            pl.BlockSpec((1, 1, VT_ROWS, s), lambda bi, kv, qi: (bi, kv, 0, 0)),
        ],
        out_specs=pl.BlockSpec((1, tq, group_w), lambda bi, kv, qi: (bi, qi, kv)),
        out_shape=jax.ShapeDtypeStruct((b, s, A_WIDTH), BF16),
        scratch_shapes=[
            pltpu.VMEM((LANES, A_GROUP * tq), BF16),
            pltpu.VMEM((8, A_GROUP * tq), F32),
            pltpu.VMEM((VT_ROWS, A_GROUP * tq), F32),
            pltpu.VMEM((tk, A_GROUP * tq), F32),
        ],
        compiler_params=_params("parallel", "parallel", "parallel"),
        name="attn_a",
    )(qat, ka, vat)


def _attn_b_kernel(q_ref, k_ref, v_ref, o_ref, lse_ref, *, dil, sub_len, rows_per_step):
    rb = ATTN_B_ROWS
    kw = rb + 2 * WINDOW_RADIUS
    step = pl.program_id(2)
    lane = lax.broadcasted_iota(jnp.int32, (rb, LANES), 1)
    upper = lane >= HEAD_DIM
    rel0 = (lax.broadcasted_iota(jnp.int32, (rb, kw), 0)
            - lax.broadcasted_iota(jnp.int32, (rb, kw), 1))
    for rt in range(rows_per_step // rb):
        j0 = step * rows_per_step + rt * rb
        ws = jnp.clip(j0 - WINDOW_RADIUS, 0, sub_len - kw)
        ws = pl.multiple_of(ws, WINDOW_RADIUS)
        rel = rel0 + (j0 - ws)
        valid = jnp.abs(rel) <= WINDOW_RADIUS
        for off in range(dil):
            cols = slice(off * LANES, (off + 1) * LANES)
            q = q_ref[0, 0, rt * rb:(rt + 1) * rb, cols]
            kwin = k_ref[0, 0, pl.ds(ws, kw), cols]
            vwin = v_ref[0, 0, pl.ds(ws, kw), cols]
            o_slab = None
            lse_slab = None
            for h in range(HEADS_PER_SLAB):
                keep = upper if h else jnp.logical_not(upper)
                qm = jnp.where(keep, q, jnp.zeros_like(q))
                s = lax.dot_general(qm, kwin, (((1,), (1,)), ((), ())),
                                    preferred_element_type=F32)
                s = jnp.where(valid, s, NEG_INF)
                m = jnp.max(s, axis=-1, keepdims=True)
                p = jnp.exp(s - m)
                den = jnp.sum(p, axis=-1, keepdims=True)
                o = jnp.dot(p.astype(BF16), vwin, preferred_element_type=F32) / den
                lse = jnp.broadcast_to(m + jnp.log(den), (rb, LANES))
                o_slab = o if h == 0 else jnp.where(upper, o, o_slab)
                lse_slab = lse if h == 0 else jnp.where(upper, lse, lse_slab)
            o_ref[0, 0, rt * rb:(rt + 1) * rb, cols] = o_slab.astype(o_ref.dtype)
            lse_ref[0, 0, rt * rb:(rt + 1) * rb, cols] = lse_slab


def _attn_b_call(q4, k4, v4, dil):
    nslab, b, s, _ = q4.shape
    sub_len = s // dil
    rows = ATTN_B_POS // dil
    view = (nslab, b, sub_len, dil * LANES)
    kern = functools.partial(_attn_b_kernel, dil=dil, sub_len=sub_len, rows_per_step=rows)
    tile = pl.BlockSpec((1, 1, rows, dil * LANES), lambda sl, bi, t: (sl, bi, t, 0))
    full = pl.BlockSpec((1, 1, sub_len, dil * LANES), lambda sl, bi, t: (sl, bi, 0, 0))
    o, lse = pl.pallas_call(
        kern,
        grid=(nslab, b, sub_len // rows),
        in_specs=[tile, full, full],
        out_specs=[tile, tile],
        out_shape=[jax.ShapeDtypeStruct(view, BF16), jax.ShapeDtypeStruct(view, F32)],
        compiler_params=_params("parallel", "parallel", "parallel"),
        name=f"attn_b_d{dil}",
    )(q4.reshape(view), k4.reshape(view), v4.reshape(view))
    return o.reshape(nslab, b * s, LANES), lse.reshape(nslab, b * s, LANES)


def _out_route_kernel(x_ref, oa_ref, o1_ref, o2_ref, o3_ref, l1_ref, l2_ref, l3_ref,
                      ga_ref, gb_ref, wout_ref, g2_ref, whi_ref, wlo_ref, rb_ref, tri_ref,
                      h_ref, route_ref, counts_ref, base_ref):
    tm = x_ref.shape[0]

    @pl.when(pl.program_id(0) == 0)
    def _():
        base_ref[...] = jnp.zeros(base_ref.shape, F32)

    oa = _rms(oa_ref[...].astype(F32), ga_ref[...])
    slabs = []
    for c in range(B_SLABS):
        l1, l2, l3 = l1_ref[c], l2_ref[c], l3_ref[c]
        lmax = jnp.maximum(jnp.maximum(l1, l2), l3)
        e1, e2, e3 = jnp.exp(l1 - lmax), jnp.exp(l2 - lmax), jnp.exp(l3 - lmax)
        mix = (e1 * o1_ref[c].astype(F32) + e2 * o2_ref[c].astype(F32)
               + e3 * o3_ref[c].astype(F32)) / (e1 + e2 + e3)
        slabs.append(mix)
    ob = _rms(jnp.concatenate(slabs, axis=-1), gb_ref[...])
    cat = jnp.concatenate([oa, ob], axis=-1).astype(BF16)
    h = x_ref[...] + jnp.dot(cat, wout_ref[...], preferred_element_type=F32)
    h_ref[...] = h

    xt = _rms(h, g2_ref[...])
    xh = xt.astype(BF16)
    xl = (xt - xh.astype(F32)).astype(BF16)
    whi = whi_ref[...]
    logits = (jnp.dot(xh, whi, preferred_element_type=F32)
              + jnp.dot(xl, whi, preferred_element_type=F32)
              + jnp.dot(xh, wlo_ref[...], preferred_element_type=F32)) + rb_ref[...]

    lane = lax.broadcasted_iota(jnp.int32, (tm, LANES), 1).astype(F32)
    none = float(LANES)

    def first_argmax(vals):
        top = jnp.max(vals, axis=-1, keepdims=True)
        idx = jnp.min(jnp.where(vals == top, lane, none), axis=-1, keepdims=True)
        return top, idx

    gl = jnp.where(lane < N_GROUPS, logits, -jnp.inf)
    gmax, gidx = first_argmax(gl)
    gprob = 1.0 / jnp.sum(jnp.exp(gl - gmax), axis=-1, keepdims=True)
    lo = N_GROUPS + EXPERTS_PER_GROUP * gidx
    el = jnp.where((lane >= lo) & (lane < lo + EXPERTS_PER_GROUP), logits, -jnp.inf)
    v1, i1 = first_argmax(el)
    v2, i2 = first_argmax(jnp.where(lane == i1, -jnp.inf, el))
    t = jnp.exp(v2 - v1)
    gate1 = gprob / (1.0 + t)
    gate2 = gprob * t / (1.0 + t)
    x1 = i1 - N_GROUPS
    x2 = i2 - N_GROUPS

    hot1 = lane == x1
    hot2 = lane == x2
    hot = jnp.where(hot1 | hot2, 1.0, 0.0)
    before = base_ref[...] + jnp.dot(tri_ref[...], hot.astype(BF16), preferred_element_type=F32)
    r1 = jnp.sum(jnp.where(hot1, before, 0.0), axis=-1, keepdims=True)
    r2 = jnp.sum(jnp.where(hot2, before, 0.0), axis=-1, keepdims=True)
    base = base_ref[...] + jnp.sum(hot, axis=0, keepdims=True)
    base_ref[...] = base
    counts_ref[...] = jnp.broadcast_to(base, counts_ref.shape)

    packed = jnp.zeros((tm, LANES), F32)
    for i, col in enumerate((x1, x2, gate1, gate2, r1, r2)):
        packed = jnp.where(lane == i, col, packed)
    route_ref[...] = packed


def _out_route_call(x2, oa, o_pats, lse_pats, ga, gb, wout, g2, whi, wlo, rbias, tri):
    n, dm = x2.shape
    tm = PROJ_ROWS
    const = lambda i: (0, 0)
    row = lambda i: (i, 0)
    slab = pl.BlockSpec((B_SLABS, tm, LANES), lambda i: (0, i, 0))
    return pl.pallas_call(
        _out_route_kernel,
        grid=(n // tm,),
        in_specs=[
            pl.BlockSpec((tm, dm), row),
            pl.BlockSpec((tm, A_WIDTH), row),
            slab, slab, slab, slab, slab, slab,
            pl.BlockSpec((1, A_WIDTH), const),
            pl.BlockSpec((1, B_WIDTH), const),
            pl.BlockSpec((A_WIDTH + B_WIDTH, dm), const),
            pl.BlockSpec((1, dm), const),
            pl.BlockSpec((dm, LANES), const),
            pl.BlockSpec((dm, LANES), const),
            pl.BlockSpec((1, LANES), const),
            pl.BlockSpec((tm, tm), const),
        ],
        out_specs=[
            pl.BlockSpec((tm, dm), row),
            pl.BlockSpec((tm, LANES), row),
            pl.BlockSpec((8, LANES), const),
        ],
        out_shape=[
            jax.ShapeDtypeStruct((n, dm), F32),
            jax.ShapeDtypeStruct((n, LANES), F32),
            jax.ShapeDtypeStruct((8, LANES), F32),
        ],
        scratch_shapes=[pltpu.VMEM((1, LANES), F32)],
        compiler_params=_params("arbitrary"),
        name="out_route",
    )(x2, oa, *o_pats, *lse_pats, ga, gb, wout, g2, whi, wlo, rbias, tri)


def _row_copy(src_ref, src_row, dst_ref, dst_row, sem):
    return pltpu.make_async_copy(src_ref.at[pl.ds(src_row, 1)], dst_ref.at[pl.ds(dst_row, 1)], sem)


def _push_kernel(dest_ref, h_ref, g2_ref, xs_in_ref, xs_ref, xt_ref, sem):
    del xs_in_ref
    tm = h_ref.shape[0]
    xt_ref[...] = _rms(h_ref[...], g2_ref[...])

    def start(i, carry):
        for c in range(2):
            _row_copy(xt_ref, i, xs_ref, dest_ref[0, 0, 2 * i + c], sem).start()
        return carry

    lax.fori_loop(0, tm, start, 0)

    def wait(i, carry):
        _row_copy(xt_ref, 0, xs_ref, 0, sem).wait()
        return carry

    lax.fori_loop(0, 2 * tm, wait, 0)


def _push_call(dest3, h, g2, xs_zero):
    n, dm = h.shape
    tm = PROJ_ROWS
    return pl.pallas_call(
        _push_kernel,
        grid=(n // tm,),
        in_specs=[
            pl.BlockSpec((1, 1, 2 * tm), lambda i: (i, 0, 0), memory_space=pltpu.SMEM),
            pl.BlockSpec((tm, dm), lambda i: (i, 0)),
            pl.BlockSpec((1, dm), lambda i: (0, 0)),
            pl.BlockSpec(memory_space=pl.ANY),
        ],
        out_specs=pl.BlockSpec(memory_space=pl.ANY),
        out_shape=jax.ShapeDtypeStruct(xs_zero.shape, F32),
        scratch_shapes=[pltpu.VMEM((tm, dm), F32), pltpu.SemaphoreType.DMA(())],
        input_output_aliases={3: 0},
        compiler_params=_params("arbitrary"),
        name="push",
    )(dest3, h, g2, xs_zero)


def _moe_kernel(be_ref, nv_ref, xs_ref, wg_ref, wu_ref, wd_ref, y_ref):
    del be_ref

    @pl.when(pl.program_id(0) < nv_ref[0])
    def _():
        xb = xs_ref[...].astype(BF16)
        a = jnp.dot(xb, wg_ref[0], preferred_element_type=F32)
        u = jnp.dot(xb, wu_ref[0], preferred_element_type=F32)
        hdn = (a / (1.0 + jnp.exp(-a))) * u
        y_ref[...] = jnp.dot(hdn.astype(BF16), wd_ref[0], preferred_element_type=F32)


def _moe_call(block_e, n_valid, xs, wg, wu, wd):
    rows, dm = xs.shape
    de = wg.shape[-1]
    blk = lambda i, be, nv: (jnp.minimum(i, nv[0] - 1), 0)
    wsel = lambda i, be, nv: (be[jnp.minimum(i, nv[0] - 1)], 0, 0)
    grid_spec = pltpu.PrefetchScalarGridSpec(
        num_scalar_prefetch=2,
        grid=(rows // MOE_BLOCK,),
        in_specs=[
            pl.BlockSpec((MOE_BLOCK, dm), blk),
            pl.BlockSpec((1, dm, de), wsel),
            pl.BlockSpec((1, dm, de), wsel),
            pl.BlockSpec((1, de, dm), wsel),
        ],
        out_specs=pl.BlockSpec((MOE_BLOCK, dm), blk),
    )
    return pl.pallas_call(
        _moe_kernel,
        grid_spec=grid_spec,
        out_shape=jax.ShapeDtypeStruct((rows, dm), F32),
        compiler_params=_params("arbitrary"),
        name="moe",
    )(block_e, n_valid, xs, wg, wu, wd)


def _final_kernel(dest_ref, h_ref, route_ref, gf_ref, yb_ref, out_ref, y1_ref, y2_ref, sem):
    tm = h_ref.shape[0]

    def start(i, carry):
        _row_copy(yb_ref, dest_ref[0, 0, 2 * i], y1_ref, i, sem).start()
        _row_copy(yb_ref, dest_ref[0, 0, 2 * i + 1], y2_ref, i, sem).start()
        return carry

    lax.fori_loop(0, tm, start, 0)

    def wait(i, carry):
        _row_copy(yb_ref, 0, y1_ref, 0, sem).wait()
        return carry

    lax.fori_loop(0, 2 * tm, wait, 0)
    route = route_ref[...]
    y = route[:, 2:3] * y1_ref[...] + route[:, 3:4] * y2_ref[...]
    out_ref[...] = _rms(h_ref[...] + y, gf_ref[...])


def _final_call(dest3, h, route, gf, yb):
    n, dm = h.shape
    tm = PROJ_ROWS
    return pl.pallas_call(
        _final_kernel,
        grid=(n // tm,),
        in_specs=[
            pl.BlockSpec((1, 1, 2 * tm), lambda i: (i, 0, 0), memory_space=pltpu.SMEM),
            pl.BlockSpec((tm, dm), lambda i: (i, 0)),
            pl.BlockSpec((tm, LANES), lambda i: (i, 0)),
            pl.BlockSpec((1, dm), lambda i: (0, 0)),
            pl.BlockSpec(memory_space=pl.ANY),
        ],
        out_specs=pl.BlockSpec((tm, dm), lambda i: (i, 0)),
        out_shape=jax.ShapeDtypeStruct((n, dm), F32),
        scratch_shapes=[pltpu.VMEM((tm, dm), F32), pltpu.VMEM((tm, dm), F32),
                        pltpu.SemaphoreType.DMA(())],
        compiler_params=_params("arbitrary"),
        name="final",
    )(dest3, h, route, gf, yb)


def _rope_inv_freq(dim):
    return 1.0 / (ROPE_THETA ** (jnp.arange(0, dim, 2, dtype=F32) / dim))


def _rotary_tables(angles):
    cos = jnp.tile(jnp.cos(angles), (1, LANES // angles.shape[1]))
    sin = jnp.sin(angles)
    sin = jnp.tile(jnp.concatenate([-sin, sin], axis=-1), (1, HEADS_PER_SLAB))
    return cos, sin


def _axial_angles(seq):
    rows = seq // GRID_W
    row = jnp.repeat(jnp.arange(rows, dtype=F32), GRID_W)
    col = jnp.tile(jnp.arange(GRID_W, dtype=F32), rows)
    f = _rope_inv_freq(HEAD_DIM // 2)
    return jnp.concatenate([row[:, None] * f, col[:, None] * f], axis=-1)


def _linear_angles(seq):
    return jnp.arange(seq, dtype=F32)[:, None] * _rope_inv_freq(HEAD_DIM)


def _extended_w_in(w):
    scale = HEAD_DIM ** -0.5
    o = 0
    qa = w[:, o:o + A_WIDTH]; o += A_WIDTH
    ka = w[:, o:o + A_KV_WIDTH]; o += A_KV_WIDTH
    va = w[:, o:o + A_KV_WIDTH]; o += A_KV_WIDTH
    qb = w[:, o:o + B_WIDTH]; o += B_WIDTH
    kb = w[:, o:o + B_WIDTH]; o += B_WIDTH
    vb = w[:, o:o + B_WIDTH]

    def dup(t):
        heads = [t[:, h * HEAD_DIM:(h + 1) * HEAD_DIM] for h in range(A_KV_HEADS)]
        return jnp.concatenate([p for h in heads for p in (h, h)], axis=1)

    return jnp.concatenate([qa, dup(ka), va, qb * scale, kb, vb], axis=1).astype(BF16)


def _layer(h2, batch, seq, p):
    n, dm = h2.shape
    scale = HEAD_DIM ** -0.5
    gqk = jnp.concatenate([jnp.tile(p["q_norm_g"] * (scale * LOG2_E), A_HEADS),
                           jnp.tile(p["k_norm_g"], 2 * A_KV_HEADS)])[None, :]
    gi = jnp.arange(QK_A_WIDTH) // HEAD_DIM
    ones_bd = (gi[:, None] == gi[None, :]).astype(BF16)
    cosa, sina = _rotary_tables(_axial_angles(seq))
    cosb, sinb = _rotary_tables(_linear_angles(seq))

    qat, ka, vat, qb, kb, vb = _proj_call(h2, p["norm1_g"][None, :], _extended_w_in(p["w_in"]),
                                          ones_bd, gqk, cosa, sina, cosb, sinb, seq)

    oa = _attn_a_call(qat, ka.reshape(batch, seq, 2 * A_KV_WIDTH), vat).reshape(n, A_WIDTH)

    shape4 = (B_SLABS, batch, seq, LANES)
    q4, k4, v4 = qb.reshape(shape4), kb.reshape(shape4), vb.reshape(shape4)
    pats = [_attn_b_call(q4, k4, v4, d) for d in DILATIONS]

    wr = jnp.concatenate([p["router_group_w"], p["router_expert_w"]], axis=1)
    wr = jnp.pad(wr, ((0, 0), (0, LANES - wr.shape[1])))
    whi = wr.astype(BF16)
    wlo = (wr - whi.astype(F32)).astype(BF16)
    rbias = jnp.concatenate([p["router_group_b"], p["router_expert_b"]])
    rbias = jnp.pad(rbias, (0, LANES - rbias.shape[0]))[None, :]
    ti = jnp.arange(PROJ_ROWS)
    tri = (ti[:, None] > ti[None, :]).astype(BF16)

    h1, route, counts = _out_route_call(
        h2, oa, [o for o, _ in pats], [l for _, l in pats],
        p["out_norm_a_g"][None, :], p["out_norm_b_g"][None, :], p["w_out"].astype(BF16),
        p["norm2_g"][None, :], whi, wlo, rbias, tri)

    counts = counts[0, :N_EXPERTS].astype(jnp.int32)
    padded = (counts + MOE_BLOCK - 1) // MOE_BLOCK * MOE_BLOCK
    pend = jnp.cumsum(padded)
    pstart = pend - padded
    expert = route[:, 0:2].astype(jnp.int32)
    dest = pstart[expert] + route[:, 4:6].astype(jnp.int32)
    dest3 = dest.reshape(n // PROJ_ROWS, 1, 2 * PROJ_ROWS)
    n_blocks = -(-(2 * n) // MOE_BLOCK) + N_EXPERTS
    block_e = jnp.minimum(jnp.searchsorted(pend, jnp.arange(n_blocks) * MOE_BLOCK, side="right"),
                          N_EXPERTS - 1).astype(jnp.int32)
    n_valid = (pend[-1:] // MOE_BLOCK).astype(jnp.int32)

    xs = _push_call(dest3, h1, p["norm2_g"][None, :],
                    jnp.zeros((n_blocks * MOE_BLOCK, dm), F32))
    yb = _moe_call(block_e, n_valid, xs, p["w_gate"].astype(BF16), p["w_up"].astype(BF16),
                   p["w_down"].astype(BF16))
    return dest3, h1, route, yb


def kernel(x, norm1_g, w_in, q_norm_g, k_norm_g, out_norm_a_g, out_norm_b_g, w_out, norm2_g,
           router_group_w, router_group_b, router_expert_w, router_expert_b, w_gate, w_up, w_down,
           final_norm_g):
    batch, seq, dm = x.shape
    assert dm == A_WIDTH + B_WIDTH and norm1_g.shape[0] == 1
    assert seq % max(ATTN_B_POS, PROJ_ROWS, ATTN_A_K) == 0
    assert seq // max(DILATIONS) >= ATTN_B_ROWS + 2 * WINDOW_RADIUS
    layer = dict(norm1_g=norm1_g[0], w_in=w_in[0], q_norm_g=q_norm_g[0], k_norm_g=k_norm_g[0],
                 out_norm_a_g=out_norm_a_g[0], out_norm_b_g=out_norm_b_g[0], w_out=w_out[0],
                 norm2_g=norm2_g[0], router_group_w=router_group_w[0],
                 router_group_b=router_group_b[0], router_expert_w=router_expert_w[0],
                 router_expert_b=router_expert_b[0], w_gate=w_gate[0], w_up=w_up[0],
                 w_down=w_down[0])
    dest3, h1, route, yb = _layer(x.reshape(batch * seq, dm), batch, seq, layer)
    out = _final_call(dest3, h1, route, final_norm_g[None, :], yb)
    return out.reshape(batch, seq, dm)
```

```python
import functools

import jax
import jax.numpy as jnp
from jax import lax
from jax.experimental import pallas as pl
from jax.experimental.pallas import tpu as pltpu

F32 = jnp.float32
BF16 = jnp.bfloat16

HEAD_DIM = 64
A_HEADS = 8
A_KV_HEADS = 2
A_GROUP = A_HEADS // A_KV_HEADS
B_HEADS = 8
A_WIDTH = A_HEADS * HEAD_DIM
A_KV_WIDTH = A_KV_HEADS * HEAD_DIM
B_WIDTH = B_HEADS * HEAD_DIM
ROPE_THETA = 10000.0
GRID_W = 64
DILATIONS = (1, 4, 16)
WINDOW_RADIUS = 64
N_GROUPS = 4
EXPERTS_PER_GROUP = 8
N_EXPERTS = N_GROUPS * EXPERTS_PER_GROUP
MOE_BLOCK = 512
EPS = 1e-6
NEG_INF = -1e30
LOG2_E = 1.4426950408889634
SHIFT_MARGIN = 1.02
MAX_FIXED_SHIFT = 50.0

LANES = 128
HEADS_PER_SLAB = LANES // HEAD_DIM
B_SLABS = B_WIDTH // LANES
VT_ROWS = HEAD_DIM + 16
VMEM_LIMIT_BYTES = 56 * 1024 * 1024

PROJ_ROWS = 512
ATTN_A_Q = 128
ATTN_A_K = 512
ATTN_B_ROWS = 128
ATTN_B_POS = 2048
ATTN_B_UNROLL = 4
DMA_UNROLL = 8

_QA = (0, A_WIDTH)
_KA = (_QA[1], _QA[1] + 2 * A_KV_WIDTH)
_VA = (_KA[1], _KA[1] + A_KV_WIDTH)
_QB = (_VA[1], _VA[1] + B_WIDTH)
_KB = (_QB[1], _QB[1] + B_WIDTH)
_VB = (_KB[1], _KB[1] + B_WIDTH)
EXT_WIDTH = _VB[1]
QK_A_WIDTH = _KA[1]


def _params(*semantics):
    return pltpu.CompilerParams(dimension_semantics=semantics,
                                vmem_limit_bytes=VMEM_LIMIT_BYTES)


def _rms(xf, g):
    return xf * lax.rsqrt(jnp.mean(xf * xf, axis=-1, keepdims=True) + EPS) * g


def _swap_half_heads(x):
    lane = lax.broadcasted_iota(jnp.int32, x.shape, 1)
    first = (lane % HEAD_DIM) < (HEAD_DIM // 2)
    return jnp.where(first, pltpu.roll(x, LANES - HEAD_DIM // 2, 1),
                     pltpu.roll(x, HEAD_DIM // 2, 1))


def _rotary(x, cos, sin_signed):
    return x * cos + _swap_half_heads(x) * sin_signed


def _proj_kernel(x_ref, g1_ref, w_ref, ones_ref, gqk_ref, cosa_ref, sina_ref, cosb_ref, sinb_ref,
                 qat_ref, ka_ref, vat_ref, qb_ref, kb_ref, vb_ref):
    tm = x_ref.shape[0]
    u = _rms(x_ref[...], g1_ref[...]).astype(BF16)

    def proj(cols):
        return jnp.dot(u, w_ref[:, cols[0]:cols[1]], preferred_element_type=F32)

    a = proj((0, QK_A_WIDTH))
    ss = jnp.dot((a * a).astype(BF16), ones_ref[...], preferred_element_type=F32)
    an = a * lax.rsqrt(ss * (1.0 / HEAD_DIM) + EPS) * gqk_ref[...]
    cosa, sina = cosa_ref[...], sina_ref[...]
    for c in range(QK_A_WIDTH // LANES):
        r = _rotary(an[:, c * LANES:(c + 1) * LANES], cosa, sina)
        if c < A_WIDTH // LANES:
            qat_ref[0, c * LANES:(c + 1) * LANES, :] = r.T.astype(BF16)
        else:
            c2 = c - A_WIDTH // LANES
            lane = lax.broadcasted_iota(jnp.int32, r.shape, 1)
            r = jnp.where(lane < HEAD_DIM, r, jnp.where(lane == HEAD_DIM, 1.0, 0.0))
            ka_ref[:, c2 * LANES:(c2 + 1) * LANES] = r.astype(BF16)
    vat = proj(_VA).T
    row = lax.broadcasted_iota(jnp.int32, (VT_ROWS - HEAD_DIM, tm), 0)
    ones_row = jnp.where(row == 0, 1.0, 0.0).astype(BF16)
    for kv in range(A_KV_HEADS):
        vat_ref[0, kv, 0:HEAD_DIM, :] = vat[kv * HEAD_DIM:(kv + 1) * HEAD_DIM].astype(BF16)
        vat_ref[0, kv, HEAD_DIM:VT_ROWS, :] = ones_row

    cosb, sinb = cosb_ref[...], sinb_ref[...]
    for cols, out_ref in ((_QB, qb_ref), (_KB, kb_ref)):
        t = proj(cols)
        for c in range(B_SLABS):
            out_ref[c] = _rotary(t[:, c * LANES:(c + 1) * LANES], cosb, sinb)
    t = proj(_VB)
    for c in range(B_SLABS):
        vb_ref[c] = t[:, c * LANES:(c + 1) * LANES]


def _proj_call(x2, g1, w_ext, ones_bd, gqk, cosa, sina, cosb, sinb, seq):
    n, dm = x2.shape
    tm = PROJ_ROWS
    pos_blocks = seq // tm
    const = lambda i: (0, 0)
    row = lambda i: (i, 0)
    pos = lambda i: (i % pos_blocks, 0)
    slab = lambda i: (0, i, 0)
    batch = n // seq
    return pl.pallas_call(
        _proj_kernel,
        grid=(n // tm,),
        in_specs=[
            pl.BlockSpec((tm, dm), row),
            pl.BlockSpec((1, dm), const),
            pl.BlockSpec((dm, EXT_WIDTH), const),
            pl.BlockSpec((QK_A_WIDTH, QK_A_WIDTH), const),
            pl.BlockSpec((1, QK_A_WIDTH), const),
            pl.BlockSpec((tm, LANES), pos),
            pl.BlockSpec((tm, LANES), pos),
            pl.BlockSpec((tm, LANES), pos),
            pl.BlockSpec((tm, LANES), pos),
        ],
        out_specs=[
            pl.BlockSpec((1, A_WIDTH, tm), lambda i: (i // pos_blocks, 0, i % pos_blocks)),
            pl.BlockSpec((tm, 2 * A_KV_WIDTH), row),
            pl.BlockSpec((1, A_KV_HEADS, VT_ROWS, tm),
                         lambda i: (i // pos_blocks, 0, 0, i % pos_blocks)),
            pl.BlockSpec((B_SLABS, tm, LANES), slab),
            pl.BlockSpec((B_SLABS, tm, LANES), slab),
            pl.BlockSpec((B_SLABS, tm, LANES), slab),
        ],
        out_shape=[
            jax.ShapeDtypeStruct((batch, A_WIDTH, seq), BF16),
            jax.ShapeDtypeStruct((n, 2 * A_KV_WIDTH), BF16),
            jax.ShapeDtypeStruct((batch, A_KV_HEADS, VT_ROWS, seq), BF16),
            jax.ShapeDtypeStruct((B_SLABS, n, LANES), F32),
            jax.ShapeDtypeStruct((B_SLABS, n, LANES), F32),
            jax.ShapeDtypeStruct((B_SLABS, n, LANES), F32),
        ],
        compiler_params=_params("parallel"),
        name="proj",
    )(x2, g1, w_ext, ones_bd, gqk, cosa, sina, cosb, sinb)


def _attn_a_kernel(qt_ref, k_ref, vt_ref, o_ref, w_ref, m_ref, acc_ref, s_ref, *, tq, tk, nk):
    nq = A_GROUP * tq
    w_ref[HEAD_DIM:, :] = jnp.zeros((LANES - HEAD_DIM, nq), BF16)
    for g in range(A_GROUP):
        w_ref[0:HEAD_DIM, g * tq:(g + 1) * tq] = qt_ref[0, g * HEAD_DIM:(g + 1) * HEAD_DIM, :]
    m_ref[...] = jnp.full(m_ref.shape, NEG_INF, F32)
    acc_ref[...] = jnp.zeros(acc_ref.shape, F32)

    def chunk_start(c):
        return c * tk if isinstance(c, int) else pl.multiple_of(c * tk, tk)

    def scores(c):
        k = k_ref[0, pl.ds(chunk_start(c), tk), :]
        return jnp.dot(k, w_ref[...], preferred_element_type=F32)

    def accumulate(s, c):
        vt = vt_ref[0, 0, :, pl.ds(chunk_start(c), tk)]
        s3 = s.reshape(tk // 8, 8, nq)
        m_prev = m_ref[...]
        m_new = jnp.maximum(m_prev, jnp.max(jnp.max(s3, axis=0), axis=0, keepdims=True))
        p = jnp.exp2(s3 - m_new[None]).reshape(tk, nq).astype(BF16)
        acc = acc_ref[...].reshape(VT_ROWS // 8, 8, nq) * jnp.exp2(m_prev - m_new)[None]
        acc_ref[...] = acc.reshape(VT_ROWS, nq) + jnp.dot(vt, p, preferred_element_type=F32)
        m_ref[...] = m_new

    def pair(c0, last):
        s1 = scores(c0 + 1)
        accumulate(s_ref[...], c0)
        if not last:
            s_ref[...] = scores(c0 + 2)
        accumulate(s1, c0 + 1)

    s_ref[...] = scores(0)

    def body(i, carry):
        pair(2 * i, False)
        return carry

    lax.fori_loop(0, nk // 2 - 1, body, 0)
    pair(nk - 2, True)
    _attn_a_store(acc_ref[...], o_ref, tq)


def _attn_a_shift_kernel(shift_ref, qt_ref, k_ref, vt_ref, o_ref, w_ref, *, tq, tk, nk):
    nq = A_GROUP * tq
    q_all = jnp.concatenate(
        [qt_ref[0, g * HEAD_DIM:(g + 1) * HEAD_DIM, :] for g in range(A_GROUP)], axis=1)
    row = lax.broadcasted_iota(jnp.int32, (16, nq), 0)
    shift_rows = jnp.where(row == 0, -shift_ref[0], 0.0).astype(BF16)
    w_ref[...] = jnp.concatenate(
        [q_all, shift_rows, jnp.zeros((LANES - HEAD_DIM - 16, nq), BF16)], axis=0)

    def scores(c):
        return jnp.dot(k_ref[0, c * tk:(c + 1) * tk, :], w_ref[...], preferred_element_type=F32)

    acc = jnp.zeros((VT_ROWS, nq), F32)
    s = scores(0)
    for c in range(nk):
        s_next = scores(c + 1) if c + 1 < nk else None
        p = jnp.exp2(s).astype(BF16)
        acc = acc + jnp.dot(vt_ref[0, 0, :, c * tk:(c + 1) * tk], p, preferred_element_type=F32)
        s = s_next
    _attn_a_store(acc, o_ref, tq)


def _attn_a_store(acc, o_ref, tq):
    ot = acc[0:HEAD_DIM] / acc[HEAD_DIM:HEAD_DIM + 1]
    for j in range(A_GROUP // 2):
        pair = jnp.concatenate([ot[:, (2 * j) * tq:(2 * j + 1) * tq],
                                ot[:, (2 * j + 1) * tq:(2 * j + 2) * tq]], axis=0)
        o_ref[0, :, j * LANES:(j + 1) * LANES] = pair.T.astype(o_ref.dtype)


def _attn_a_call(qat, ka, vat, shift):
    b, _, s = qat.shape
    tq, tk = ATTN_A_Q, ATTN_A_K
    group_w = A_GROUP * HEAD_DIM
    nq = A_GROUP * tq
    in_specs = [
        pl.BlockSpec((1, group_w, tq), lambda bi, kv, qi: (bi, kv, qi)),
        pl.BlockSpec((1, s, LANES), lambda bi, kv, qi: (bi, 0, kv)),
        pl.BlockSpec((1, 1, VT_ROWS, s), lambda bi, kv, qi: (bi, kv, 0, 0)),
    ]
    common = dict(
        grid=(b, A_KV_HEADS, s // tq),
        out_specs=pl.BlockSpec((1, tq, group_w), lambda bi, kv, qi: (bi, qi, kv)),
        out_shape=jax.ShapeDtypeStruct((b, s, A_WIDTH), BF16),
        compiler_params=_params("parallel", "parallel", "parallel"),
    )

    def fixed_shift():
        return pl.pallas_call(
            functools.partial(_attn_a_shift_kernel, tq=tq, tk=tk, nk=s // tk),
            in_specs=[pl.BlockSpec(memory_space=pltpu.SMEM)] + in_specs,
            scratch_shapes=[pltpu.VMEM((LANES, nq), BF16)],
            name="attn_a_shift", **common)(shift, qat, ka, vat)

    def running_max():
        return pl.pallas_call(
            functools.partial(_attn_a_kernel, tq=tq, tk=tk, nk=s // tk),
            in_specs=in_specs,
            scratch_shapes=[
                pltpu.VMEM((LANES, nq), BF16),
                pltpu.VMEM((8, nq), F32),
                pltpu.VMEM((VT_ROWS, nq), F32),
                pltpu.VMEM((tk, nq), F32),
            ],
            name="attn_a", **common)(qat, ka, vat)

    return lax.cond(shift[0] <= MAX_FIXED_SHIFT, fixed_shift, running_max)


def _attn_b_kernel(q_ref, k_ref, v_ref, o_ref, op_ref, lp_ref, *, seq):
    rb = ATTN_B_ROWS
    kw = rb + 2 * WINDOW_RADIUS
    step = pl.program_id(2)
    lane = lax.broadcasted_iota(jnp.int32, (rb, LANES), 1)
    upper = lane >= HEAD_DIM
    rel0 = (lax.broadcasted_iota(jnp.int32, (rb, kw), 0)
            - lax.broadcasted_iota(jnp.int32, (rb, kw), 1))
    blocks = ATTN_B_POS // rb

    def window_block(pi, dil, u):
        sub_len = seq // dil
        rt, off = u // dil, u % dil
        j0 = step * (ATTN_B_POS // dil) + rt * rb
        ws = jnp.clip(j0 - WINDOW_RADIUS, 0, sub_len - kw)
        valid = jnp.abs(rel0 + (j0 - ws)) <= WINDOW_RADIUS
        qrows = pl.ds(rt * rb * dil + off, rb, stride=dil)
        krows = pl.ds(ws * dil + off, kw, stride=dil)
        q = q_ref[0, 0, qrows, :].astype(BF16)
        kwin = k_ref[0, 0, krows, :].astype(BF16)
        vwin = v_ref[0, 0, krows, :].astype(BF16)
        o_slab = None
        lse_slab = None
        for h in range(HEADS_PER_SLAB):
            keep = upper if h else jnp.logical_not(upper)
            qm = jnp.where(keep, q, jnp.zeros_like(q))
            s = lax.dot_general(qm, kwin, (((1,), (1,)), ((), ())),
                                preferred_element_type=F32)
            s = jnp.where(valid, s, NEG_INF)
            m = jnp.max(s, axis=-1, keepdims=True)
            p = jnp.exp(s - m)
            den = jnp.sum(p, axis=-1, keepdims=True)
            o = jnp.dot(p.astype(BF16), vwin, preferred_element_type=F32) / den
            lse = jnp.broadcast_to(m + jnp.log(den), (rb, LANES))
            o_slab = o if h == 0 else jnp.where(upper, o, o_slab)
            lse_slab = lse if h == 0 else jnp.where(upper, lse, lse_slab)
        op_ref[pi, qrows, :] = o_slab
        lp_ref[pi, qrows, :] = lse_slab

    for pi, dil in enumerate(DILATIONS):
        def trip(i, carry, pi=pi, dil=dil):
            for t in range(ATTN_B_UNROLL):
                window_block(pi, dil, i * ATTN_B_UNROLL + t)
            return carry
        lax.fori_loop(0, blocks // ATTN_B_UNROLL, trip, 0)

    l1, l2, l3 = lp_ref[0], lp_ref[1], lp_ref[2]
    lmax = jnp.maximum(jnp.maximum(l1, l2), l3)
    e1, e2, e3 = jnp.exp(l1 - lmax), jnp.exp(l2 - lmax), jnp.exp(l3 - lmax)
    mix = (e1 * op_ref[0] + e2 * op_ref[1] + e3 * op_ref[2]) / (e1 + e2 + e3)
    o_ref[0, 0] = mix.astype(o_ref.dtype)


def _attn_b_call(q4, k4, v4):
    nslab, b, s, _ = q4.shape
    tile = pl.BlockSpec((1, 1, ATTN_B_POS, LANES), lambda sl, bi, t: (sl, bi, t, 0))
    full = pl.BlockSpec((1, 1, s, LANES), lambda sl, bi, t: (sl, bi, 0, 0))
    npat = len(DILATIONS)
    return pl.pallas_call(
        functools.partial(_attn_b_kernel, seq=s),
        grid=(nslab, b, s // ATTN_B_POS),
        in_specs=[tile, full, full],
        out_specs=tile,
        out_shape=jax.ShapeDtypeStruct((nslab, b, s, LANES), BF16),
        scratch_shapes=[pltpu.VMEM((npat, ATTN_B_POS, LANES), F32),
                        pltpu.VMEM((npat, ATTN_B_POS, LANES), F32)],
        compiler_params=_params("parallel", "parallel", "parallel"),
        name="attn_b",
    )(q4, k4, v4)


def _out_route_kernel(x_ref, oa_ref, ob_ref, ga_ref, gb_ref, wout_ref, g2_ref, whi_ref, wlo_ref, rb_ref, tri_ref,
                      h_ref, route_ref, counts_ref, base_ref):
    tm = x_ref.shape[0]

    @pl.when(pl.program_id(0) == 0)
    def _():
        base_ref[...] = jnp.zeros(base_ref.shape, F32)

    oa = _rms(oa_ref[...].astype(F32), ga_ref[...])
    ob = jnp.concatenate([ob_ref[c].astype(F32) for c in range(B_SLABS)], axis=-1)
    ob = _rms(ob, gb_ref[...])
    cat = jnp.concatenate([oa, ob], axis=-1).astype(BF16)
    h = x_ref[...] + jnp.dot(cat, wout_ref[...], preferred_element_type=F32)
    h_ref[...] = h

    xt = _rms(h, g2_ref[...])
    xh = xt.astype(BF16)
    xl = (xt - xh.astype(F32)).astype(BF16)
    whi = whi_ref[...]
    logits = (jnp.dot(xh, whi, preferred_element_type=F32)
              + jnp.dot(xl, whi, preferred_element_type=F32)
              + jnp.dot(xh, wlo_ref[...], preferred_element_type=F32)) + rb_ref[...]

    lane = lax.broadcasted_iota(jnp.int32, (tm, LANES), 1).astype(F32)
    none = float(LANES)

    def first_argmax(vals):
        top = jnp.max(vals, axis=-1, keepdims=True)
        idx = jnp.min(jnp.where(vals == top, lane, none), axis=-1, keepdims=True)
        return top, idx

    gl = jnp.where(lane < N_GROUPS, logits, -jnp.inf)
    gmax, gidx = first_argmax(gl)
    gprob = 1.0 / jnp.sum(jnp.exp(gl - gmax), axis=-1, keepdims=True)
    lo = N_GROUPS + EXPERTS_PER_GROUP * gidx
    el = jnp.where((lane >= lo) & (lane < lo + EXPERTS_PER_GROUP), logits, -jnp.inf)
    v1, i1 = first_argmax(el)
    v2, i2 = first_argmax(jnp.where(lane == i1, -jnp.inf, el))
    t = jnp.exp(v2 - v1)
    gate1 = gprob / (1.0 + t)
    gate2 = gprob * t / (1.0 + t)
    x1 = i1 - N_GROUPS
    x2 = i2 - N_GROUPS

    hot1 = lane == x1
    hot2 = lane == x2
    hot = jnp.where(hot1 | hot2, 1.0, 0.0)
    before = base_ref[...] + jnp.dot(tri_ref[...], hot.astype(BF16), preferred_element_type=F32)
    r1 = jnp.sum(jnp.where(hot1, before, 0.0), axis=-1, keepdims=True)
    r2 = jnp.sum(jnp.where(hot2, before, 0.0), axis=-1, keepdims=True)
    base = base_ref[...] + jnp.sum(hot, axis=0, keepdims=True)
    base_ref[...] = base
    counts_ref[...] = jnp.broadcast_to(base, counts_ref.shape)

    packed = jnp.zeros((tm, LANES), F32)
    for i, col in enumerate((x1, x2, gate1, gate2, r1, r2)):
        packed = jnp.where(lane == i, col, packed)
    route_ref[...] = packed


def _out_route_call(x2, oa, ob, ga, gb, wout, g2, whi, wlo, rbias, tri):
    n, dm = x2.shape
    tm = PROJ_ROWS
    const = lambda i: (0, 0)
    row = lambda i: (i, 0)
    slab = pl.BlockSpec((B_SLABS, tm, LANES), lambda i: (0, i, 0))
    return pl.pallas_call(
        _out_route_kernel,
        grid=(n // tm,),
        in_specs=[
            pl.BlockSpec((tm, dm), row),
            pl.BlockSpec((tm, A_WIDTH), row),
            slab,
            pl.BlockSpec((1, A_WIDTH), const),
            pl.BlockSpec((1, B_WIDTH), const),
            pl.BlockSpec((A_WIDTH + B_WIDTH, dm), const),
            pl.BlockSpec((1, dm), const),
            pl.BlockSpec((dm, LANES), const),
            pl.BlockSpec((dm, LANES), const),
            pl.BlockSpec((1, LANES), const),
            pl.BlockSpec((tm, tm), const),
        ],
        out_specs=[
            pl.BlockSpec((tm, dm), row),
            pl.BlockSpec((tm, LANES), row),
            pl.BlockSpec((8, LANES), const),
        ],
        out_shape=[
            jax.ShapeDtypeStruct((n, dm), F32),
            jax.ShapeDtypeStruct((n, LANES), F32),
            jax.ShapeDtypeStruct((8, LANES), F32),
        ],
        scratch_shapes=[pltpu.VMEM((1, LANES), F32)],
        compiler_params=_params("arbitrary"),
        name="out_route",
    )(x2, oa, ob, ga, gb, wout, g2, whi, wlo, rbias, tri)


def _row_copy(src_ref, src_row, dst_ref, dst_row, sem):
    return pltpu.make_async_copy(src_ref.at[pl.ds(src_row, 1)], dst_ref.at[pl.ds(dst_row, 1)], sem)


def _block_copy(src_ref, dst_ref, dst_row, sem):
    return pltpu.make_async_copy(src_ref, dst_ref.at[pl.ds(dst_row, MOE_BLOCK)], sem)


def _push_kernel(tail_ref, dest_ref, h_ref, g2_ref, xs_ref, xt_ref, zero_ref, sem):
    tm = h_ref.shape[0]

    @pl.when(pl.program_id(0) == 0)
    def _():
        zero_ref[...] = jnp.zeros(zero_ref.shape, F32)
        for e in range(tail_ref.shape[0]):
            @pl.when(tail_ref[e] >= 0)
            def _():
                _block_copy(zero_ref, xs_ref, pl.multiple_of(tail_ref[e], MOE_BLOCK), sem).start()
        for e in range(tail_ref.shape[0]):
            @pl.when(tail_ref[e] >= 0)
            def _():
                _block_copy(zero_ref, xs_ref, pl.multiple_of(tail_ref[e], MOE_BLOCK), sem).wait()

    xt_ref[...] = _rms(h_ref[...], g2_ref[...])

    def start(i, carry):
        for c in range(2):
            _row_copy(xt_ref, i, xs_ref, dest_ref[0, 0, 2 * i + c], sem).start()
        return carry

    lax.fori_loop(0, tm, start, 0, unroll=DMA_UNROLL)

    def wait(i, carry):
        _row_copy(xt_ref, 0, xs_ref, 0, sem).wait()
        return carry

    lax.fori_loop(0, 2 * tm, wait, 0, unroll=DMA_UNROLL)


def _push_call(tail_rows, dest3, h, g2, rows):
    n, dm = h.shape
    tm = PROJ_ROWS
    return pl.pallas_call(
        _push_kernel,
        grid=(n // tm,),
        in_specs=[
            pl.BlockSpec(memory_space=pltpu.SMEM),
            pl.BlockSpec((1, 1, 2 * tm), lambda i: (i, 0, 0), memory_space=pltpu.SMEM),
            pl.BlockSpec((tm, dm), lambda i: (i, 0)),
            pl.BlockSpec((1, dm), lambda i: (0, 0)),
        ],
        out_specs=pl.BlockSpec(memory_space=pl.ANY),
        out_shape=jax.ShapeDtypeStruct((rows, dm), F32),
        scratch_shapes=[pltpu.VMEM((tm, dm), F32), pltpu.VMEM((MOE_BLOCK, dm), F32),
                        pltpu.SemaphoreType.DMA(())],
        compiler_params=_params("arbitrary"),
        name="push",
    )(tail_rows, dest3, h, g2)


def _moe_kernel(be_ref, nv_ref, xs_ref, wg_ref, wu_ref, wd_ref, y_ref):
    del be_ref
    used = pl.program_id(0) < nv_ref[0]

    @pl.when(jnp.logical_not(used))
    def _():
        y_ref[...] = jnp.zeros(y_ref.shape, F32)

    @pl.when(used)
    def _():
        xb = xs_ref[...].astype(BF16)
        a = jnp.dot(xb, wg_ref[0], preferred_element_type=F32)
        u = jnp.dot(xb, wu_ref[0], preferred_element_type=F32)
        hdn = (a / (1.0 + jnp.exp(-a))) * u
        y_ref[...] = jnp.dot(hdn.astype(BF16), wd_ref[0], preferred_element_type=F32)


def _moe_call(block_e, n_valid, xs, wg, wu, wd):
    rows, dm = xs.shape
    de = wg.shape[-1]
    blk = lambda i, be, nv: (jnp.minimum(i, nv[0] - 1), 0)
    wsel = lambda i, be, nv: (be[jnp.minimum(i, nv[0] - 1)], 0, 0)
    grid_spec = pltpu.PrefetchScalarGridSpec(
        num_scalar_prefetch=2,
        grid=(rows // MOE_BLOCK,),
        in_specs=[
            pl.BlockSpec((MOE_BLOCK, dm), blk),
            pl.BlockSpec((1, dm, de), wsel),
            pl.BlockSpec((1, dm, de), wsel),
            pl.BlockSpec((1, de, dm), wsel),
        ],
        out_specs=pl.BlockSpec((MOE_BLOCK, dm), lambda i, be, nv: (i, 0)),
    )
    return pl.pallas_call(
        _moe_kernel,
        grid_spec=grid_spec,
        out_shape=jax.ShapeDtypeStruct((rows, dm), F32),
        compiler_params=_params("arbitrary"),
        name="moe",
    )(block_e, n_valid, xs, wg, wu, wd)


def _final_kernel(dest_ref, h_ref, route_ref, gf_ref, yb_ref, out_ref, y1_ref, y2_ref, sem):
    tm = h_ref.shape[0]

    def start(i, carry):
        _row_copy(yb_ref, dest_ref[0, 0, 2 * i], y1_ref, i, sem).start()
        _row_copy(yb_ref, dest_ref[0, 0, 2 * i + 1], y2_ref, i, sem).start()
        return carry

    lax.fori_loop(0, tm, start, 0, unroll=DMA_UNROLL)

    def wait(i, carry):
        _row_copy(yb_ref, 0, y1_ref, 0, sem).wait()
        return carry

    lax.fori_loop(0, 2 * tm, wait, 0, unroll=DMA_UNROLL)
    route = route_ref[...]
    y = route[:, 2:3] * y1_ref[...] + route[:, 3:4] * y2_ref[...]
    out_ref[...] = _rms(h_ref[...] + y, gf_ref[...])


def _final_call(dest3, h, route, gf, yb):
    n, dm = h.shape
    tm = PROJ_ROWS
    return pl.pallas_call(
        _final_kernel,
        grid=(n // tm,),
        in_specs=[
            pl.BlockSpec((1, 1, 2 * tm), lambda i: (i, 0, 0), memory_space=pltpu.SMEM),
            pl.BlockSpec((tm, dm), lambda i: (i, 0)),
            pl.BlockSpec((tm, LANES), lambda i: (i, 0)),
            pl.BlockSpec((1, dm), lambda i: (0, 0)),
            pl.BlockSpec(memory_space=pl.ANY),
        ],
        out_specs=pl.BlockSpec((tm, dm), lambda i: (i, 0)),
        out_shape=jax.ShapeDtypeStruct((n, dm), F32),
        scratch_shapes=[pltpu.VMEM((tm, dm), F32), pltpu.VMEM((tm, dm), F32),
                        pltpu.SemaphoreType.DMA(())],
        compiler_params=_params("arbitrary"),
        name="final",
    )(dest3, h, route, gf, yb)


def _rope_inv_freq(dim):
    return 1.0 / (ROPE_THETA ** (jnp.arange(0, dim, 2, dtype=F32) / dim))


def _rotary_tables(angles):
    cos = jnp.tile(jnp.cos(angles), (1, LANES // angles.shape[1]))
    sin = jnp.sin(angles)
    sin = jnp.tile(jnp.concatenate([-sin, sin], axis=-1), (1, HEADS_PER_SLAB))
    return cos, sin


def _axial_angles(seq):
    rows = seq // GRID_W
    row = jnp.repeat(jnp.arange(rows, dtype=F32), GRID_W)
    col = jnp.tile(jnp.arange(GRID_W, dtype=F32), rows)
    f = _rope_inv_freq(HEAD_DIM // 2)
    return jnp.concatenate([row[:, None] * f, col[:, None] * f], axis=-1)


def _linear_angles(seq):
    return jnp.arange(seq, dtype=F32)[:, None] * _rope_inv_freq(HEAD_DIM)


def _extended_w_in(w):
    scale = HEAD_DIM ** -0.5
    o = 0
    qa = w[:, o:o + A_WIDTH]; o += A_WIDTH
    ka = w[:, o:o + A_KV_WIDTH]; o += A_KV_WIDTH
    va = w[:, o:o + A_KV_WIDTH]; o += A_KV_WIDTH
    qb = w[:, o:o + B_WIDTH]; o += B_WIDTH
    kb = w[:, o:o + B_WIDTH]; o += B_WIDTH
    vb = w[:, o:o + B_WIDTH]

    def dup(t):
        heads = [t[:, h * HEAD_DIM:(h + 1) * HEAD_DIM] for h in range(A_KV_HEADS)]
        return jnp.concatenate([p for h in heads for p in (h, h)], axis=1)

    return jnp.concatenate([qa, dup(ka), va, qb * scale, kb, vb], axis=1).astype(BF16)


def _layer(h2, batch, seq, p):
    n, dm = h2.shape
    scale = HEAD_DIM ** -0.5
    gqk = jnp.concatenate([jnp.tile(p["q_norm_g"] * (scale * LOG2_E), A_HEADS),
                           jnp.tile(p["k_norm_g"], 2 * A_KV_HEADS)])[None, :]
    gi = jnp.arange(QK_A_WIDTH) // HEAD_DIM
    ones_bd = (gi[:, None] == gi[None, :]).astype(BF16)
    cosa, sina = _rotary_tables(_axial_angles(seq))
    cosb, sinb = _rotary_tables(_linear_angles(seq))

    qat, ka, vat, qb, kb, vb = _proj_call(h2, p["norm1_g"][None, :], _extended_w_in(p["w_in"]),
                                          ones_bd, gqk, cosa, sina, cosb, sinb, seq)

    shift = (HEAD_DIM * scale * LOG2_E * SHIFT_MARGIN
             * jnp.max(jnp.abs(p["q_norm_g"])) * jnp.max(jnp.abs(p["k_norm_g"])))
    shift = shift.astype(BF16).astype(F32)[None]
    oa = _attn_a_call(qat, ka.reshape(batch, seq, 2 * A_KV_WIDTH), vat, shift).reshape(n, A_WIDTH)

    shape4 = (B_SLABS, batch, seq, LANES)
    q4, k4, v4 = qb.reshape(shape4), kb.reshape(shape4), vb.reshape(shape4)
    ob = _attn_b_call(q4, k4, v4).reshape(B_SLABS, n, LANES)

    wr = jnp.concatenate([p["router_group_w"], p["router_expert_w"]], axis=1)
    wr = jnp.pad(wr, ((0, 0), (0, LANES - wr.shape[1])))
    whi = wr.astype(BF16)
    wlo = (wr - whi.astype(F32)).astype(BF16)
    rbias = jnp.concatenate([p["router_group_b"], p["router_expert_b"]])
    rbias = jnp.pad(rbias, (0, LANES - rbias.shape[0]))[None, :]
    ti = jnp.arange(PROJ_ROWS)
    tri = (ti[:, None] > ti[None, :]).astype(BF16)

    h1, route, counts = _out_route_call(
        h2, oa, ob, p["out_norm_a_g"][None, :], p["out_norm_b_g"][None, :], p["w_out"].astype(BF16),
        p["norm2_g"][None, :], whi, wlo, rbias, tri)

    counts = counts[0, :N_EXPERTS].astype(jnp.int32)
    padded = (counts + MOE_BLOCK - 1) // MOE_BLOCK * MOE_BLOCK
    pend = jnp.cumsum(padded)
    pstart = pend - padded
    expert = route[:, 0:2].astype(jnp.int32)
    dest = pstart[expert] + route[:, 4:6].astype(jnp.int32)
    dest3 = dest.reshape(n // PROJ_ROWS, 1, 2 * PROJ_ROWS)
    n_blocks = -(-(2 * n) // MOE_BLOCK) + N_EXPERTS
    block_row = jnp.arange(n_blocks, dtype=jnp.int32) * MOE_BLOCK
    block_e = jnp.minimum(jnp.sum((block_row[:, None] >= pend[None, :]).astype(jnp.int32), axis=1),
                          N_EXPERTS - 1)
    n_valid = (pend[-1:] // MOE_BLOCK).astype(jnp.int32)
    unused = pend[-1] + jnp.arange(N_EXPERTS, dtype=jnp.int32) * MOE_BLOCK
    tail_rows = jnp.concatenate([jnp.where(padded > 0, pend - MOE_BLOCK, -1),
                                 jnp.where(unused < n_blocks * MOE_BLOCK, unused, -1)])
    tail_rows = tail_rows.astype(jnp.int32)

    xs = _push_call(tail_rows, dest3, h1, p["norm2_g"][None, :], n_blocks * MOE_BLOCK)
    yb = _moe_call(block_e, n_valid, xs, p["w_gate"].astype(BF16), p["w_up"].astype(BF16),
                   p["w_down"].astype(BF16))
    return dest3, h1, route, yb


def kernel(x, norm1_g, w_in, q_norm_g, k_norm_g, out_norm_a_g, out_norm_b_g, w_out, norm2_g,
           router_group_w, router_group_b, router_expert_w, router_expert_b, w_gate, w_up, w_down,
           final_norm_g):
    batch, seq, dm = x.shape
    assert dm == A_WIDTH + B_WIDTH and norm1_g.shape[0] == 1
    assert seq % max(ATTN_B_POS, PROJ_ROWS, ATTN_A_K) == 0
    assert seq // max(DILATIONS) >= ATTN_B_ROWS + 2 * WINDOW_RADIUS
    layer = dict(norm1_g=norm1_g[0], w_in=w_in[0], q_norm_g=q_norm_g[0], k_norm_g=k_norm_g[0],
                 out_norm_a_g=out_norm_a_g[0], out_norm_b_g=out_norm_b_g[0], w_out=w_out[0],
                 norm2_g=norm2_g[0], router_group_w=router_group_w[0],
                 router_group_b=router_group_b[0], router_expert_w=router_expert_w[0],
                 router_expert_b=router_expert_b[0], w_gate=w_gate[0], w_up=w_up[0],
                 w_down=w_down[0])
    dest3, h1, route, yb = _layer(x.reshape(batch * seq, dm), batch, seq, layer)
    out = _final_call(dest3, h1, route, final_norm_g[None, :], yb)
    return out.reshape(batch, seq, dm)
```

```python
import functools

import jax
import jax.numpy as jnp
from jax import lax
from jax.experimental import pallas as pl
from jax.experimental.pallas import tpu as pltpu

F32 = jnp.float32
BF16 = jnp.bfloat16

HEAD_DIM = 64
A_HEADS = 8
A_KV_HEADS = 2
A_GROUP = A_HEADS // A_KV_HEADS
B_HEADS = 8
A_WIDTH = A_HEADS * HEAD_DIM
A_KV_WIDTH = A_KV_HEADS * HEAD_DIM
B_WIDTH = B_HEADS * HEAD_DIM
ROPE_THETA = 10000.0
GRID_W = 64
DILATIONS = (1, 4, 16)
WINDOW_RADIUS = 64
N_GROUPS = 4
EXPERTS_PER_GROUP = 8
N_EXPERTS = N_GROUPS * EXPERTS_PER_GROUP
MOE_BLOCK = 512
EPS = 1e-6
NEG_INF = -1e30
LOG2_E = 1.4426950408889634
SHIFT_MARGIN = 1.02
MAX_FIXED_SHIFT = 50.0

LANES = 128
HEADS_PER_SLAB = LANES // HEAD_DIM
B_SLABS = B_WIDTH // LANES
VT_ROWS = HEAD_DIM + 16
VMEM_LIMIT_BYTES = 56 * 1024 * 1024

PROJ_ROWS = 512
ATTN_A_Q = 128
ATTN_A_K = 512
ATTN_B_ROWS = 128
ATTN_B_POS = 2048
ATTN_B_UNROLL = 8
ATTN_B_REGROUP_ROWS = 512
DMA_UNROLL = 8

_QA = (0, A_WIDTH)
_KA = (_QA[1], _QA[1] + 2 * A_KV_WIDTH)
_VA = (_KA[1], _KA[1] + A_KV_WIDTH)
_QB = (_VA[1], _VA[1] + B_WIDTH)
_KB = (_QB[1], _QB[1] + B_WIDTH)
_VB = (_KB[1], _KB[1] + B_WIDTH)
EXT_WIDTH = _VB[1]
QK_A_WIDTH = _KA[1]


def _params(*semantics):
    return pltpu.CompilerParams(dimension_semantics=semantics,
                                vmem_limit_bytes=VMEM_LIMIT_BYTES)


def _rms(xf, g):
    return xf * lax.rsqrt(jnp.mean(xf * xf, axis=-1, keepdims=True) + EPS) * g


def _swap_half_heads(x):
    lane = lax.broadcasted_iota(jnp.int32, x.shape, 1)
    first = (lane % HEAD_DIM) < (HEAD_DIM // 2)
    return jnp.where(first, pltpu.roll(x, LANES - HEAD_DIM // 2, 1),
                     pltpu.roll(x, HEAD_DIM // 2, 1))


def _rotary(x, cos, sin_signed):
    return x * cos + _swap_half_heads(x) * sin_signed


def _proj_kernel(x_ref, g1_ref, w_ref, ones_ref, gqk_ref, cosa_ref, sina_ref, cosb_ref, sinb_ref,
                 qat_ref, ka_ref, vat_ref, qb_ref, kb_ref, vb_ref):
    tm = x_ref.shape[0]
    u = _rms(x_ref[...], g1_ref[...]).astype(BF16)

    def proj(cols):
        return jnp.dot(u, w_ref[:, cols[0]:cols[1]], preferred_element_type=F32)

    a = proj((0, QK_A_WIDTH))
    ss = jnp.dot((a * a).astype(BF16), ones_ref[...], preferred_element_type=F32)
    an = a * lax.rsqrt(ss * (1.0 / HEAD_DIM) + EPS) * gqk_ref[...]
    cosa, sina = cosa_ref[...], sina_ref[...]
    for c in range(QK_A_WIDTH // LANES):
        r = _rotary(an[:, c * LANES:(c + 1) * LANES], cosa, sina)
        if c < A_WIDTH // LANES:
            qat_ref[0, c * LANES:(c + 1) * LANES, :] = r.T.astype(BF16)
        else:
            c2 = c - A_WIDTH // LANES
            lane = lax.broadcasted_iota(jnp.int32, r.shape, 1)
            r = jnp.where(lane < HEAD_DIM, r, jnp.where(lane == HEAD_DIM, 1.0, 0.0))
            ka_ref[:, c2 * LANES:(c2 + 1) * LANES] = r.astype(BF16)
    vat = proj(_VA).T
    row = lax.broadcasted_iota(jnp.int32, (VT_ROWS - HEAD_DIM, tm), 0)
    ones_row = jnp.where(row == 0, 1.0, 0.0).astype(BF16)
    for kv in range(A_KV_HEADS):
        vat_ref[0, kv, 0:HEAD_DIM, :] = vat[kv * HEAD_DIM:(kv + 1) * HEAD_DIM].astype(BF16)
        vat_ref[0, kv, HEAD_DIM:VT_ROWS, :] = ones_row

    cosb, sinb = cosb_ref[...], sinb_ref[...]
    for cols, out_ref in ((_QB, qb_ref), (_KB, kb_ref)):
        t = proj(cols)
        for c in range(B_SLABS):
            out_ref[c] = _rotary(t[:, c * LANES:(c + 1) * LANES], cosb, sinb)
    t = proj(_VB)
    for c in range(B_SLABS):
        vb_ref[c] = t[:, c * LANES:(c + 1) * LANES]


def _proj_call(x2, g1, w_ext, ones_bd, gqk, cosa, sina, cosb, sinb, seq):
    n, dm = x2.shape
    tm = PROJ_ROWS
    pos_blocks = seq // tm
    const = lambda i: (0, 0)
    row = lambda i: (i, 0)
    pos = lambda i: (i % pos_blocks, 0)
    slab = lambda i: (0, i, 0)
    batch = n // seq
    return pl.pallas_call(
        _proj_kernel,
        grid=(n // tm,),
        in_specs=[
            pl.BlockSpec((tm, dm), row),
            pl.BlockSpec((1, dm), const),
            pl.BlockSpec((dm, EXT_WIDTH), const),
            pl.BlockSpec((QK_A_WIDTH, QK_A_WIDTH), const),
            pl.BlockSpec((1, QK_A_WIDTH), const),
            pl.BlockSpec((tm, LANES), pos),
            pl.BlockSpec((tm, LANES), pos),
            pl.BlockSpec((tm, LANES), pos),
            pl.BlockSpec((tm, LANES), pos),
        ],
        out_specs=[
            pl.BlockSpec((1, A_WIDTH, tm), lambda i: (i // pos_blocks, 0, i % pos_blocks)),
            pl.BlockSpec((tm, 2 * A_KV_WIDTH), row),
            pl.BlockSpec((1, A_KV_HEADS, VT_ROWS, tm),
                         lambda i: (i // pos_blocks, 0, 0, i % pos_blocks)),
            pl.BlockSpec((B_SLABS, tm, LANES), slab),
            pl.BlockSpec((B_SLABS, tm, LANES), slab),
            pl.BlockSpec((B_SLABS, tm, LANES), slab),
        ],
        out_shape=[
            jax.ShapeDtypeStruct((batch, A_WIDTH, seq), BF16),
            jax.ShapeDtypeStruct((n, 2 * A_KV_WIDTH), BF16),
            jax.ShapeDtypeStruct((batch, A_KV_HEADS, VT_ROWS, seq), BF16),
            jax.ShapeDtypeStruct((B_SLABS, n, LANES), F32),
            jax.ShapeDtypeStruct((B_SLABS, n, LANES), F32),
            jax.ShapeDtypeStruct((B_SLABS, n, LANES), F32),
        ],
        compiler_params=_params("parallel"),
        name="proj",
    )(x2, g1, w_ext, ones_bd, gqk, cosa, sina, cosb, sinb)


def _attn_a_kernel(qt_ref, k_ref, vt_ref, o_ref, w_ref, m_ref, acc_ref, s_ref, *, tq, tk, nk):
    nq = A_GROUP * tq
    w_ref[HEAD_DIM:, :] = jnp.zeros((LANES - HEAD_DIM, nq), BF16)
    for g in range(A_GROUP):
        w_ref[0:HEAD_DIM, g * tq:(g + 1) * tq] = qt_ref[0, g * HEAD_DIM:(g + 1) * HEAD_DIM, :]
    m_ref[...] = jnp.full(m_ref.shape, NEG_INF, F32)
    acc_ref[...] = jnp.zeros(acc_ref.shape, F32)

    def chunk_start(c):
        return c * tk if isinstance(c, int) else pl.multiple_of(c * tk, tk)

    def scores(c):
        k = k_ref[0, pl.ds(chunk_start(c), tk), :]
        return jnp.dot(k, w_ref[...], preferred_element_type=F32)

    def accumulate(s, c):
        vt = vt_ref[0, 0, :, pl.ds(chunk_start(c), tk)]
        s3 = s.reshape(tk // 8, 8, nq)
        m_prev = m_ref[...]
        m_new = jnp.maximum(m_prev, jnp.max(jnp.max(s3, axis=0), axis=0, keepdims=True))
        p = jnp.exp2(s3 - m_new[None]).reshape(tk, nq).astype(BF16)
        acc = acc_ref[...].reshape(VT_ROWS // 8, 8, nq) * jnp.exp2(m_prev - m_new)[None]
        acc_ref[...] = acc.reshape(VT_ROWS, nq) + jnp.dot(vt, p, preferred_element_type=F32)
        m_ref[...] = m_new

    def pair(c0, last):
        s1 = scores(c0 + 1)
        accumulate(s_ref[...], c0)
        if not last:
            s_ref[...] = scores(c0 + 2)
        accumulate(s1, c0 + 1)

    s_ref[...] = scores(0)

    def body(i, carry):
        pair(2 * i, False)
        return carry

    lax.fori_loop(0, nk // 2 - 1, body, 0)
    pair(nk - 2, True)
    _attn_a_store(acc_ref[...], o_ref, tq)


def _attn_a_shift_kernel(shift_ref, qt_ref, k_ref, vt_ref, o_ref, w_ref, *, tq, tk, nk):
    nq = A_GROUP * tq
    q_all = jnp.concatenate(
        [qt_ref[0, g * HEAD_DIM:(g + 1) * HEAD_DIM, :] for g in range(A_GROUP)], axis=1)
    row = lax.broadcasted_iota(jnp.int32, (16, nq), 0)
    shift_rows = jnp.where(row == 0, -shift_ref[0], 0.0).astype(BF16)
    w_ref[...] = jnp.concatenate(
        [q_all, shift_rows, jnp.zeros((LANES - HEAD_DIM - 16, nq), BF16)], axis=0)

    def scores(c):
        return jnp.dot(k_ref[0, c * tk:(c + 1) * tk, :], w_ref[...], preferred_element_type=F32)

    acc = jnp.zeros((VT_ROWS, nq), F32)
    s = scores(0)
    for c in range(nk):
        s_next = scores(c + 1) if c + 1 < nk else None
        p = jnp.exp2(s).astype(BF16)
        acc = acc + jnp.dot(vt_ref[0, 0, :, c * tk:(c + 1) * tk], p, preferred_element_type=F32)
        s = s_next
    _attn_a_store(acc, o_ref, tq)


def _attn_a_store(acc, o_ref, tq):
    ot = acc[0:HEAD_DIM] / acc[HEAD_DIM:HEAD_DIM + 1]
    for j in range(A_GROUP // 2):
        pair = jnp.concatenate([ot[:, (2 * j) * tq:(2 * j + 1) * tq],
                                ot[:, (2 * j + 1) * tq:(2 * j + 2) * tq]], axis=0)
        o_ref[0, :, j * LANES:(j + 1) * LANES] = pair.T.astype(o_ref.dtype)


def _attn_a_call(qat, ka, vat, shift):
    b, _, s = qat.shape
    tq, tk = ATTN_A_Q, ATTN_A_K
    group_w = A_GROUP * HEAD_DIM
    nq = A_GROUP * tq
    in_specs = [
        pl.BlockSpec((1, group_w, tq), lambda bi, kv, qi: (bi, kv, qi)),
        pl.BlockSpec((1, s, LANES), lambda bi, kv, qi: (bi, 0, kv)),
        pl.BlockSpec((1, 1, VT_ROWS, s), lambda bi, kv, qi: (bi, kv, 0, 0)),
    ]
    common = dict(
        grid=(b, A_KV_HEADS, s // tq),
        out_specs=pl.BlockSpec((1, tq, group_w), lambda bi, kv, qi: (bi, qi, kv)),
        out_shape=jax.ShapeDtypeStruct((b, s, A_WIDTH), BF16),
        compiler_params=_params("parallel", "parallel", "parallel"),
    )

    def fixed_shift():
        return pl.pallas_call(
            functools.partial(_attn_a_shift_kernel, tq=tq, tk=tk, nk=s // tk),
            in_specs=[pl.BlockSpec(memory_space=pltpu.SMEM)] + in_specs,
            scratch_shapes=[pltpu.VMEM((LANES, nq), BF16)],
            name="attn_a_shift", **common)(shift, qat, ka, vat)

    def running_max():
        return pl.pallas_call(
            functools.partial(_attn_a_kernel, tq=tq, tk=tk, nk=s // tk),
            in_specs=in_specs,
            scratch_shapes=[
                pltpu.VMEM((LANES, nq), BF16),
                pltpu.VMEM((8, nq), F32),
                pltpu.VMEM((VT_ROWS, nq), F32),
                pltpu.VMEM((tk, nq), F32),
            ],
            name="attn_a", **common)(qat, ka, vat)

    return lax.cond(shift[0] <= MAX_FIXED_SHIFT, fixed_shift, running_max)


def _attn_b_kernel(q_ref, k_ref, v_ref, o_ref, kd_ref, vd_ref, bias_ref, op_ref, lp_ref, *, seq):
    rb = ATTN_B_ROWS
    kw = rb + 2 * WINDOW_RADIUS
    step = pl.program_id(2)
    lane = lax.broadcasted_iota(jnp.int32, (rb, LANES), 1)
    upper = lane >= HEAD_DIM
    blocks = ATTN_B_POS // rb

    @pl.when(step == 0)
    def _():
        rel = (lax.broadcasted_iota(jnp.int32, (rb, kw), 0)
               - lax.broadcasted_iota(jnp.int32, (rb, kw), 1))
        for case in range(3):
            ok = jnp.abs(rel + WINDOW_RADIUS * case) <= WINDOW_RADIUS
            bias_ref[case] = jnp.where(ok, 0.0, NEG_INF)
        chunk = min(ATTN_B_REGROUP_ROWS, seq // max(DILATIONS))
        for pi, dil in enumerate(DILATIONS):
            per_off = seq // dil // chunk

            def regroup(u, carry, pi=pi, dil=dil, per_off=per_off):
                off, c = u // per_off, u % per_off
                src = pl.ds(c * chunk * dil + off, chunk, stride=dil)
                dst = pl.ds(pl.multiple_of(u * chunk, chunk), chunk)
                kd_ref[pi, dst, :] = k_ref[0, 0, src, :].astype(BF16)
                vd_ref[pi, dst, :] = v_ref[0, 0, src, :].astype(BF16)
                return carry

            lax.fori_loop(0, seq // chunk, regroup, 0)

    def block_scores(pi, dil, u):
        sub_len = seq // dil
        rt, off = u // dil, u % dil
        j0 = step * (ATTN_B_POS // dil) + rt * rb
        ws = jnp.clip(j0 - WINDOW_RADIUS, 0, sub_len - kw)
        qrows = pl.ds(rt * rb * dil + off, rb, stride=dil)
        krows = pl.ds(pl.multiple_of(off * sub_len + ws, WINDOW_RADIUS), kw)
        q = q_ref[0, 0, qrows, :].astype(BF16)
        zero = jnp.zeros_like(q)
        qs = jnp.concatenate([jnp.where(upper, zero, q), jnp.where(upper, q, zero)], axis=0)
        bias = bias_ref[(j0 - ws) // WINDOW_RADIUS]
        s = lax.dot_general(qs, kd_ref[pi, krows, :], (((1,), (1,)), ((), ())),
                            preferred_element_type=F32)
        return s + jnp.concatenate([bias, bias], axis=0), qrows, krows

    def block_output(pi, s, qrows, krows):
        m = jnp.max(s, axis=-1, keepdims=True)
        p = jnp.exp2(s - m)
        den = jnp.sum(p, axis=-1, keepdims=True)
        o = jnp.dot(p.astype(BF16), vd_ref[pi, krows, :], preferred_element_type=F32) / den
        l2 = jnp.broadcast_to(m + jnp.log2(den), (HEADS_PER_SLAB * rb, LANES))
        op_ref[pi, qrows, :] = jnp.where(upper, o[rb:], o[:rb])
        lp_ref[pi, qrows, :] = jnp.where(upper, l2[rb:], l2[:rb])

    for pi, dil in enumerate(DILATIONS):
        def trip(i, carry, pi=pi, dil=dil):
            scored = [block_scores(pi, dil, i * ATTN_B_UNROLL + t) for t in range(ATTN_B_UNROLL)]
            for s, qrows, krows in scored:
                block_output(pi, s, qrows, krows)
            return carry
        lax.fori_loop(0, blocks // ATTN_B_UNROLL, trip, 0)

    l1, l2, l3 = lp_ref[0], lp_ref[1], lp_ref[2]
    lmax = jnp.maximum(jnp.maximum(l1, l2), l3)
    e1, e2, e3 = jnp.exp2(l1 - lmax), jnp.exp2(l2 - lmax), jnp.exp2(l3 - lmax)
    mix = (e1 * op_ref[0] + e2 * op_ref[1] + e3 * op_ref[2]) / (e1 + e2 + e3)
    o_ref[0, 0] = mix.astype(o_ref.dtype)


def _attn_b_call(q4, k4, v4):
    nslab, b, s, _ = q4.shape
    tile = pl.BlockSpec((1, 1, ATTN_B_POS, LANES), lambda sl, bi, t: (sl, bi, t, 0))
    full = pl.BlockSpec((1, 1, s, LANES), lambda sl, bi, t: (sl, bi, 0, 0))
    npat = len(DILATIONS)
    kw = ATTN_B_ROWS + 2 * WINDOW_RADIUS
    return pl.pallas_call(
        functools.partial(_attn_b_kernel, seq=s),
        grid=(nslab, b, s // ATTN_B_POS),
        in_specs=[tile, full, full],
        out_specs=tile,
        out_shape=jax.ShapeDtypeStruct((nslab, b, s, LANES), BF16),
        scratch_shapes=[pltpu.VMEM((npat, s, LANES), BF16),
                        pltpu.VMEM((npat, s, LANES), BF16),
                        pltpu.VMEM((3, ATTN_B_ROWS, kw), F32),
                        pltpu.VMEM((npat, ATTN_B_POS, LANES), F32),
                        pltpu.VMEM((npat, ATTN_B_POS, LANES), F32)],
        compiler_params=_params("parallel", "parallel", "arbitrary"),
        name="attn_b",
    )(q4, k4, v4)


def _out_route_kernel(x_ref, oa_ref, ob_ref, ga_ref, gb_ref, wout_ref, g2_ref, whi_ref, wlo_ref, rb_ref, tri_ref,
                      h_ref, route_ref, counts_ref, base_ref):
    tm = x_ref.shape[0]

    @pl.when(pl.program_id(0) == 0)
    def _():
        base_ref[...] = jnp.zeros(base_ref.shape, F32)

    oa = _rms(oa_ref[...].astype(F32), ga_ref[...])
    ob = jnp.concatenate([ob_ref[c].astype(F32) for c in range(B_SLABS)], axis=-1)
    ob = _rms(ob, gb_ref[...])
    cat = jnp.concatenate([oa, ob], axis=-1).astype(BF16)
    h = x_ref[...] + jnp.dot(cat, wout_ref[...], preferred_element_type=F32)
    h_ref[...] = h

    xt = _rms(h, g2_ref[...])
    xh = xt.astype(BF16)
    xl = (xt - xh.astype(F32)).astype(BF16)
    whi = whi_ref[...]
    logits = (jnp.dot(xh, whi, preferred_element_type=F32)
              + jnp.dot(xl, whi, preferred_element_type=F32)
              + jnp.dot(xh, wlo_ref[...], preferred_element_type=F32)) + rb_ref[...]

    lane = lax.broadcasted_iota(jnp.int32, (tm, LANES), 1).astype(F32)
    none = float(LANES)

    def first_argmax(vals):
        top = jnp.max(vals, axis=-1, keepdims=True)
        idx = jnp.min(jnp.where(vals == top, lane, none), axis=-1, keepdims=True)
        return top, idx

    gl = jnp.where(lane < N_GROUPS, logits, -jnp.inf)
    gmax, gidx = first_argmax(gl)
    gprob = 1.0 / jnp.sum(jnp.exp(gl - gmax), axis=-1, keepdims=True)
    lo = N_GROUPS + EXPERTS_PER_GROUP * gidx
    el = jnp.where((lane >= lo) & (lane < lo + EXPERTS_PER_GROUP), logits, -jnp.inf)
    v1, i1 = first_argmax(el)
    v2, i2 = first_argmax(jnp.where(lane == i1, -jnp.inf, el))
    t = jnp.exp(v2 - v1)
    gate1 = gprob / (1.0 + t)
    gate2 = gprob * t / (1.0 + t)
    x1 = i1 - N_GROUPS
    x2 = i2 - N_GROUPS

    hot1 = lane == x1
    hot2 = lane == x2
    hot = jnp.where(hot1 | hot2, 1.0, 0.0)
    before = base_ref[...] + jnp.dot(tri_ref[...], hot.astype(BF16), preferred_element_type=F32)
    r1 = jnp.sum(jnp.where(hot1, before, 0.0), axis=-1, keepdims=True)
    r2 = jnp.sum(jnp.where(hot2, before, 0.0), axis=-1, keepdims=True)
    base = base_ref[...] + jnp.sum(hot, axis=0, keepdims=True)
    base_ref[...] = base
    counts_ref[...] = jnp.broadcast_to(base, counts_ref.shape)

    packed = jnp.zeros((tm, LANES), F32)
    for i, col in enumerate((x1, x2, gate1, gate2, r1, r2)):
        packed = jnp.where(lane == i, col, packed)
    route_ref[...] = packed


def _out_route_call(x2, oa, ob, ga, gb, wout, g2, whi, wlo, rbias, tri):
    n, dm = x2.shape
    tm = PROJ_ROWS
    const = lambda i: (0, 0)
    row = lambda i: (i, 0)
    slab = pl.BlockSpec((B_SLABS, tm, LANES), lambda i: (0, i, 0))
    return pl.pallas_call(
        _out_route_kernel,
        grid=(n // tm,),
        in_specs=[
            pl.BlockSpec((tm, dm), row),
            pl.BlockSpec((tm, A_WIDTH), row),
            slab,
            pl.BlockSpec((1, A_WIDTH), const),
            pl.BlockSpec((1, B_WIDTH), const),
            pl.BlockSpec((A_WIDTH + B_WIDTH, dm), const),
            pl.BlockSpec((1, dm), const),
            pl.BlockSpec((dm, LANES), const),
            pl.BlockSpec((dm, LANES), const),
            pl.BlockSpec((1, LANES), const),
            pl.BlockSpec((tm, tm), const),
        ],
        out_specs=[
            pl.BlockSpec((tm, dm), row),
            pl.BlockSpec((tm, LANES), row),
            pl.BlockSpec((8, LANES), const),
        ],
        out_shape=[
            jax.ShapeDtypeStruct((n, dm), F32),
            jax.ShapeDtypeStruct((n, LANES), F32),
            jax.ShapeDtypeStruct((8, LANES), F32),
        ],
        scratch_shapes=[pltpu.VMEM((1, LANES), F32)],
        compiler_params=_params("arbitrary"),
        name="out_route",
    )(x2, oa, ob, ga, gb, wout, g2, whi, wlo, rbias, tri)


def _row_copy(src_ref, src_row, dst_ref, dst_row, sem):
    return pltpu.make_async_copy(src_ref.at[pl.ds(src_row, 1)], dst_ref.at[pl.ds(dst_row, 1)], sem)


def _block_copy(src_ref, dst_ref, dst_row, sem):
    return pltpu.make_async_copy(src_ref, dst_ref.at[pl.ds(dst_row, MOE_BLOCK)], sem)


def _push_kernel(tail_ref, dest_ref, h_ref, g2_ref, xs_ref, xt_ref, zero_ref, sem):
    tm = h_ref.shape[0]

    @pl.when(pl.program_id(0) == 0)
    def _():
        zero_ref[...] = jnp.zeros(zero_ref.shape, F32)
        for e in range(tail_ref.shape[0]):
            @pl.when(tail_ref[e] >= 0)
            def _():
                _block_copy(zero_ref, xs_ref, pl.multiple_of(tail_ref[e], MOE_BLOCK), sem).start()
        for e in range(tail_ref.shape[0]):
            @pl.when(tail_ref[e] >= 0)
            def _():
                _block_copy(zero_ref, xs_ref, pl.multiple_of(tail_ref[e], MOE_BLOCK), sem).wait()

    xt_ref[...] = _rms(h_ref[...], g2_ref[...])

    def start(i, carry):
        for c in range(2):
            _row_copy(xt_ref, i, xs_ref, dest_ref[0, 0, 2 * i + c], sem).start()
        return carry

    lax.fori_loop(0, tm, start, 0, unroll=DMA_UNROLL)

    def wait(i, carry):
        _row_copy(xt_ref, 0, xs_ref, 0, sem).wait()
        return carry

    lax.fori_loop(0, 2 * tm, wait, 0, unroll=DMA_UNROLL)


def _push_call(tail_rows, dest3, h, g2, rows):
    n, dm = h.shape
    tm = PROJ_ROWS
    return pl.pallas_call(
        _push_kernel,
        grid=(n // tm,),
        in_specs=[
            pl.BlockSpec(memory_space=pltpu.SMEM),
            pl.BlockSpec((1, 1, 2 * tm), lambda i: (i, 0, 0), memory_space=pltpu.SMEM),
            pl.BlockSpec((tm, dm), lambda i: (i, 0)),
            pl.BlockSpec((1, dm), lambda i: (0, 0)),
        ],
        out_specs=pl.BlockSpec(memory_space=pl.ANY),
        out_shape=jax.ShapeDtypeStruct((rows, dm), F32),
        scratch_shapes=[pltpu.VMEM((tm, dm), F32), pltpu.VMEM((MOE_BLOCK, dm), F32),
                        pltpu.SemaphoreType.DMA(())],
        compiler_params=_params("arbitrary"),
        name="push",
    )(tail_rows, dest3, h, g2)


def _moe_kernel(be_ref, nv_ref, xs_ref, wg_ref, wu_ref, wd_ref, y_ref):
    del be_ref
    used = pl.program_id(0) < nv_ref[0]

    @pl.when(jnp.logical_not(used))
    def _():
        y_ref[...] = jnp.zeros(y_ref.shape, F32)

    @pl.when(used)
    def _():
        xb = xs_ref[...].astype(BF16)
        a = jnp.dot(xb, wg_ref[0], preferred_element_type=F32)
        u = jnp.dot(xb, wu_ref[0], preferred_element_type=F32)
        hdn = (a / (1.0 + jnp.exp(-a))) * u
        y_ref[...] = jnp.dot(hdn.astype(BF16), wd_ref[0], preferred_element_type=F32)


def _moe_call(block_e, n_valid, xs, wg, wu, wd):
    rows, dm = xs.shape
    de = wg.shape[-1]
    blk = lambda i, be, nv: (jnp.minimum(i, nv[0] - 1), 0)
    wsel = lambda i, be, nv: (be[jnp.minimum(i, nv[0] - 1)], 0, 0)
    grid_spec = pltpu.PrefetchScalarGridSpec(
        num_scalar_prefetch=2,
        grid=(rows // MOE_BLOCK,),
        in_specs=[
            pl.BlockSpec((MOE_BLOCK, dm), blk),
            pl.BlockSpec((1, dm, de), wsel),
            pl.BlockSpec((1, dm, de), wsel),
            pl.BlockSpec((1, de, dm), wsel),
        ],
        out_specs=pl.BlockSpec((MOE_BLOCK, dm), lambda i, be, nv: (i, 0)),
    )
    return pl.pallas_call(
        _moe_kernel,
        grid_spec=grid_spec,
        out_shape=jax.ShapeDtypeStruct((rows, dm), F32),
        compiler_params=_params("arbitrary"),
        name="moe",
    )(block_e, n_valid, xs, wg, wu, wd)


def _final_kernel(dest_ref, h_ref, route_ref, gf_ref, yb_ref, out_ref, y1_ref, y2_ref, sem):
    tm = h_ref.shape[0]

    def start(i, carry):
        _row_copy(yb_ref, dest_ref[0, 0, 2 * i], y1_ref, i, sem).start()
        _row_copy(yb_ref, dest_ref[0, 0, 2 * i + 1], y2_ref, i, sem).start()
        return carry

    lax.fori_loop(0, tm, start, 0, unroll=DMA_UNROLL)

    def wait(i, carry):
        _row_copy(yb_ref, 0, y1_ref, 0, sem).wait()
        return carry

    lax.fori_loop(0, 2 * tm, wait, 0, unroll=DMA_UNROLL)
    route = route_ref[...]
    y = route[:, 2:3] * y1_ref[...] + route[:, 3:4] * y2_ref[...]
    out_ref[...] = _rms(h_ref[...] + y, gf_ref[...])


def _final_call(dest3, h, route, gf, yb):
    n, dm = h.shape
    tm = PROJ_ROWS
    return pl.pallas_call(
        _final_kernel,
        grid=(n // tm,),
        in_specs=[
            pl.BlockSpec((1, 1, 2 * tm), lambda i: (i, 0, 0), memory_space=pltpu.SMEM),
            pl.BlockSpec((tm, dm), lambda i: (i, 0)),
            pl.BlockSpec((tm, LANES), lambda i: (i, 0)),
            pl.BlockSpec((1, dm), lambda i: (0, 0)),
            pl.BlockSpec(memory_space=pl.ANY),
        ],
        out_specs=pl.BlockSpec((tm, dm), lambda i: (i, 0)),
        out_shape=jax.ShapeDtypeStruct((n, dm), F32),
        scratch_shapes=[pltpu.VMEM((tm, dm), F32), pltpu.VMEM((tm, dm), F32),
                        pltpu.SemaphoreType.DMA(())],
        compiler_params=_params("arbitrary"),
        name="final",
    )(dest3, h, route, gf, yb)


def _rope_inv_freq(dim):
    return 1.0 / (ROPE_THETA ** (jnp.arange(0, dim, 2, dtype=F32) / dim))


def _rotary_tables(angles):
    cos = jnp.tile(jnp.cos(angles), (1, LANES // angles.shape[1]))
    sin = jnp.sin(angles)
    sin = jnp.tile(jnp.concatenate([-sin, sin], axis=-1), (1, HEADS_PER_SLAB))
    return cos, sin


def _axial_angles(seq):
    rows = seq // GRID_W
    row = jnp.repeat(jnp.arange(rows, dtype=F32), GRID_W)
    col = jnp.tile(jnp.arange(GRID_W, dtype=F32), rows)
    f = _rope_inv_freq(HEAD_DIM // 2)
    return jnp.concatenate([row[:, None] * f, col[:, None] * f], axis=-1)


def _linear_angles(seq):
    return jnp.arange(seq, dtype=F32)[:, None] * _rope_inv_freq(HEAD_DIM)


def _extended_w_in(w):
    scale = HEAD_DIM ** -0.5
    o = 0
    qa = w[:, o:o + A_WIDTH]; o += A_WIDTH
    ka = w[:, o:o + A_KV_WIDTH]; o += A_KV_WIDTH
    va = w[:, o:o + A_KV_WIDTH]; o += A_KV_WIDTH
    qb = w[:, o:o + B_WIDTH]; o += B_WIDTH
    kb = w[:, o:o + B_WIDTH]; o += B_WIDTH
    vb = w[:, o:o + B_WIDTH]

    def dup(t):
        heads = [t[:, h * HEAD_DIM:(h + 1) * HEAD_DIM] for h in range(A_KV_HEADS)]
        return jnp.concatenate([p for h in heads for p in (h, h)], axis=1)

    return jnp.concatenate([qa, dup(ka), va, qb * (scale * LOG2_E), kb, vb], axis=1).astype(BF16)


def _layer(h2, batch, seq, p):
    n, dm = h2.shape
    scale = HEAD_DIM ** -0.5
    gqk = jnp.concatenate([jnp.tile(p["q_norm_g"] * (scale * LOG2_E), A_HEADS),
                           jnp.tile(p["k_norm_g"], 2 * A_KV_HEADS)])[None, :]
    gi = jnp.arange(QK_A_WIDTH) // HEAD_DIM
    ones_bd = (gi[:, None] == gi[None, :]).astype(BF16)
    cosa, sina = _rotary_tables(_axial_angles(seq))
    cosb, sinb = _rotary_tables(_linear_angles(seq))

    qat, ka, vat, qb, kb, vb = _proj_call(h2, p["norm1_g"][None, :], _extended_w_in(p["w_in"]),
                                          ones_bd, gqk, cosa, sina, cosb, sinb, seq)

    shift = (HEAD_DIM * scale * LOG2_E * SHIFT_MARGIN
             * jnp.max(jnp.abs(p["q_norm_g"])) * jnp.max(jnp.abs(p["k_norm_g"])))
    shift = shift.astype(BF16).astype(F32)[None]
    oa = _attn_a_call(qat, ka.reshape(batch, seq, 2 * A_KV_WIDTH), vat, shift).reshape(n, A_WIDTH)

    shape4 = (B_SLABS, batch, seq, LANES)
    q4, k4, v4 = qb.reshape(shape4), kb.reshape(shape4), vb.reshape(shape4)
    ob = _attn_b_call(q4, k4, v4).reshape(B_SLABS, n, LANES)

    wr = jnp.concatenate([p["router_group_w"], p["router_expert_w"]], axis=1)
    wr = jnp.pad(wr, ((0, 0), (0, LANES - wr.shape[1])))
    whi = wr.astype(BF16)
    wlo = (wr - whi.astype(F32)).astype(BF16)
    rbias = jnp.concatenate([p["router_group_b"], p["router_expert_b"]])
    rbias = jnp.pad(rbias, (0, LANES - rbias.shape[0]))[None, :]
    ti = jnp.arange(PROJ_ROWS)
    tri = (ti[:, None] > ti[None, :]).astype(BF16)

    h1, route, counts = _out_route_call(
        h2, oa, ob, p["out_norm_a_g"][None, :], p["out_norm_b_g"][None, :], p["w_out"].astype(BF16),
        p["norm2_g"][None, :], whi, wlo, rbias, tri)

    counts = counts[0, :N_EXPERTS].astype(jnp.int32)
    padded = (counts + MOE_BLOCK - 1) // MOE_BLOCK * MOE_BLOCK
    pend = jnp.cumsum(padded)
    pstart = pend - padded
    expert = route[:, 0:2].astype(jnp.int32)
    hot = expert[:, :, None] == jnp.arange(N_EXPERTS, dtype=jnp.int32)
    dest = jnp.sum(jnp.where(hot, pstart, 0), axis=-1) + route[:, 4:6].astype(jnp.int32)
    dest3 = dest.reshape(n // PROJ_ROWS, 1, 2 * PROJ_ROWS)
    n_blocks = -(-(2 * n) // MOE_BLOCK) + N_EXPERTS
    block_row = jnp.arange(n_blocks, dtype=jnp.int32) * MOE_BLOCK
    block_e = jnp.minimum(jnp.sum((block_row[:, None] >= pend[None, :]).astype(jnp.int32), axis=1),
                          N_EXPERTS - 1)
    n_valid = (pend[-1:] // MOE_BLOCK).astype(jnp.int32)
    unused = pend[-1] + jnp.arange(N_EXPERTS, dtype=jnp.int32) * MOE_BLOCK
    tail_rows = jnp.concatenate([jnp.where(padded > 0, pend - MOE_BLOCK, -1),
                                 jnp.where(unused < n_blocks * MOE_BLOCK, unused, -1)])
    tail_rows = tail_rows.astype(jnp.int32)

    xs = _push_call(tail_rows, dest3, h1, p["norm2_g"][None, :], n_blocks * MOE_BLOCK)
    yb = _moe_call(block_e, n_valid, xs, p["w_gate"].astype(BF16), p["w_up"].astype(BF16),
                   p["w_down"].astype(BF16))
    return dest3, h1, route, yb


def kernel(x, norm1_g, w_in, q_norm_g, k_norm_g, out_norm_a_g, out_norm_b_g, w_out, norm2_g,
           router_group_w, router_group_b, router_expert_w, router_expert_b, w_gate, w_up, w_down,
           final_norm_g):
    batch, seq, dm = x.shape
    assert dm == A_WIDTH + B_WIDTH and norm1_g.shape[0] == 1
    assert seq % max(ATTN_B_POS, PROJ_ROWS, ATTN_A_K) == 0
    assert seq // max(DILATIONS) >= ATTN_B_ROWS + 2 * WINDOW_RADIUS
    layer = dict(norm1_g=norm1_g[0], w_in=w_in[0], q_norm_g=q_norm_g[0], k_norm_g=k_norm_g[0],
                 out_norm_a_g=out_norm_a_g[0], out_norm_b_g=out_norm_b_g[0], w_out=w_out[0],
                 norm2_g=norm2_g[0], router_group_w=router_group_w[0],
                 router_group_b=router_group_b[0], router_expert_w=router_expert_w[0],
                 router_expert_b=router_expert_b[0], w_gate=w_gate[0], w_up=w_up[0],
                 w_down=w_down[0])
    dest3, h1, route, yb = _layer(x.reshape(batch * seq, dm), batch, seq, layer)
    out = _final_call(dest3, h1, route, final_norm_g[None, :], yb)
    return out.reshape(batch, seq, dm)
```

```python
import functools

import jax
import jax.numpy as jnp
from jax import lax
from jax.experimental import pallas as pl
from jax.experimental.pallas import tpu as pltpu

F32 = jnp.float32
BF16 = jnp.bfloat16

HEAD_DIM = 64
A_HEADS = 8
A_KV_HEADS = 2
A_GROUP = A_HEADS // A_KV_HEADS
B_HEADS = 8
A_WIDTH = A_HEADS * HEAD_DIM
A_KV_WIDTH = A_KV_HEADS * HEAD_DIM
B_WIDTH = B_HEADS * HEAD_DIM
ROPE_THETA = 10000.0
GRID_W = 64
DILATIONS = (1, 4, 16)
WINDOW_RADIUS = 64
N_GROUPS = 4
EXPERTS_PER_GROUP = 8
N_EXPERTS = N_GROUPS * EXPERTS_PER_GROUP
MOE_BLOCK = 512
EPS = 1e-6
NEG_INF = -1e30
LOG2_E = 1.4426950408889634
SHIFT_MARGIN = 1.02
MAX_FIXED_SHIFT = 50.0

LANES = 128
SUBLANES = 8
HEADS_PER_SLAB = LANES // HEAD_DIM
B_SLABS = B_WIDTH // LANES
VT_ROWS = HEAD_DIM + 16
VMEM_LIMIT_BYTES = 56 * 1024 * 1024

PROJ_ROWS = 512
ATTN_A_Q = 128
ATTN_A_K = 512
ATTN_B_ROWS = 128
ATTN_B_POS = 2048
ATTN_B_UNROLL = 8
MXU_TILE = 256

_QA = (0, A_WIDTH)
_KA = (_QA[1], _QA[1] + 2 * A_KV_WIDTH)
_VA = (_KA[1], _KA[1] + A_KV_WIDTH)
_QB = (_VA[1], _VA[1] + B_WIDTH)
_KB = (_QB[1], _QB[1] + B_WIDTH)
_VB = (_KB[1], _KB[1] + B_WIDTH)
EXT_WIDTH = _VB[1]
QK_A_WIDTH = _KA[1]


def _params(*semantics):
    return pltpu.CompilerParams(dimension_semantics=semantics,
                                vmem_limit_bytes=VMEM_LIMIT_BYTES)


def _rms(xf, g):
    return xf * lax.rsqrt(jnp.mean(xf * xf, axis=-1, keepdims=True) + EPS) * g


def _swap_half_heads(x):
    lane = lax.broadcasted_iota(jnp.int32, x.shape, 1)
    first = (lane % HEAD_DIM) < (HEAD_DIM // 2)
    return jnp.where(first, pltpu.roll(x, LANES - HEAD_DIM // 2, 1),
                     pltpu.roll(x, HEAD_DIM // 2, 1))


def _rotary(x, cos, sin_signed):
    return x * cos + _swap_half_heads(x) * sin_signed


def _proj_kernel(x_ref, g1_ref, w_ref, ones_ref, gqk_ref, cosa_ref, sina_ref, cosb_ref, sinb_ref,
                 qat_ref, ka_ref, vat_ref, qb_ref, *rest):
    npat = len(DILATIONS)
    kd_refs, vd_refs = rest[:npat], rest[npat:2 * npat]
    kst_ref, vst_ref = rest[2 * npat:]
    tm = x_ref.shape[0]
    u = _rms(x_ref[...], g1_ref[...]).astype(BF16)

    def proj(cols):
        return jnp.dot(u, w_ref[:, cols[0]:cols[1]], preferred_element_type=F32)

    a = proj((0, QK_A_WIDTH))
    sq = (a * a).astype(BF16)
    ss = jnp.concatenate(
        [jnp.dot(sq[:, j:j + MXU_TILE], ones_ref[...], preferred_element_type=F32)
         for j in range(0, QK_A_WIDTH, MXU_TILE)], axis=1)
    an = a * lax.rsqrt(ss * (1.0 / HEAD_DIM) + EPS) * gqk_ref[...]
    cosa, sina = cosa_ref[...], sina_ref[...]
    for c in range(QK_A_WIDTH // LANES):
        r = _rotary(an[:, c * LANES:(c + 1) * LANES], cosa, sina)
        if c < A_WIDTH // LANES:
            qat_ref[0, c * LANES:(c + 1) * LANES, :] = r.T.astype(BF16)
        else:
            c2 = c - A_WIDTH // LANES
            lane = lax.broadcasted_iota(jnp.int32, r.shape, 1)
            r = jnp.where(lane < HEAD_DIM, r, jnp.where(lane == HEAD_DIM, 1.0, 0.0))
            ka_ref[:, c2 * LANES:(c2 + 1) * LANES] = r.astype(BF16)
    vat = proj(_VA).T
    row = lax.broadcasted_iota(jnp.int32, (VT_ROWS - HEAD_DIM, tm), 0)
    ones_row = jnp.where(row == 0, 1.0, 0.0).astype(BF16)
    for kv in range(A_KV_HEADS):
        vat_ref[0, kv, 0:HEAD_DIM, :] = vat[kv * HEAD_DIM:(kv + 1) * HEAD_DIM].astype(BF16)
        vat_ref[0, kv, HEAD_DIM:VT_ROWS, :] = ones_row

    cosb, sinb = cosb_ref[...], sinb_ref[...]
    t = proj(_QB)
    for c in range(B_SLABS):
        qb_ref[c] = _rotary(t[:, c * LANES:(c + 1) * LANES], cosb, sinb)

    def store_regrouped(val, st_ref, outs, c):
        st_ref[c] = val
        for d, out_ref in zip(DILATIONS, outs):
            for off in range(d):
                rows = val if d == 1 else st_ref[c, pl.ds(off, tm // d, stride=d), :]
                out_ref[c, 0, off] = rows.astype(BF16)

    t = proj(_KB)
    for c in range(B_SLABS):
        store_regrouped(_rotary(t[:, c * LANES:(c + 1) * LANES], cosb, sinb), kst_ref, kd_refs, c)
    t = proj(_VB)
    for c in range(B_SLABS):
        store_regrouped(t[:, c * LANES:(c + 1) * LANES], vst_ref, vd_refs, c)


def _proj_call(x2, g1, w_ext, ones_bd, gqk, cosa, sina, cosb, sinb, seq):
    n, dm = x2.shape
    tm = PROJ_ROWS
    pos_blocks = seq // tm
    const = lambda i: (0, 0)
    row = lambda i: (i, 0)
    pos = lambda i: (i % pos_blocks, 0)
    slab = lambda i: (0, i, 0)
    batch = n // seq
    regrouped_specs = [
        pl.BlockSpec((B_SLABS, 1, d, tm // d, LANES),
                     lambda i: (0, i // pos_blocks, 0, i % pos_blocks, 0))
        for d in DILATIONS]
    regrouped_shapes = [jax.ShapeDtypeStruct((B_SLABS, batch, d, seq // d, LANES), BF16)
                        for d in DILATIONS]
    return pl.pallas_call(
        _proj_kernel,
        grid=(n // tm,),
        in_specs=[
            pl.BlockSpec((tm, dm), row),
            pl.BlockSpec((1, dm), const),
            pl.BlockSpec((dm, EXT_WIDTH), const),
            pl.BlockSpec((MXU_TILE, MXU_TILE), const),
            pl.BlockSpec((1, QK_A_WIDTH), const),
            pl.BlockSpec((tm, LANES), pos),
            pl.BlockSpec((tm, LANES), pos),
            pl.BlockSpec((tm, LANES), pos),
            pl.BlockSpec((tm, LANES), pos),
        ],
        out_specs=[
            pl.BlockSpec((1, A_WIDTH, tm), lambda i: (i // pos_blocks, 0, i % pos_blocks)),
            pl.BlockSpec((tm, 2 * A_KV_WIDTH), row),
            pl.BlockSpec((1, A_KV_HEADS, VT_ROWS, tm),
                         lambda i: (i // pos_blocks, 0, 0, i % pos_blocks)),
            pl.BlockSpec((B_SLABS, tm, LANES), slab),
        ] + regrouped_specs + regrouped_specs,
        out_shape=[
            jax.ShapeDtypeStruct((batch, A_WIDTH, seq), BF16),
            jax.ShapeDtypeStruct((n, 2 * A_KV_WIDTH), BF16),
            jax.ShapeDtypeStruct((batch, A_KV_HEADS, VT_ROWS, seq), BF16),
            jax.ShapeDtypeStruct((B_SLABS, n, LANES), F32),
        ] + regrouped_shapes + regrouped_shapes,
        scratch_shapes=[pltpu.VMEM((B_SLABS, tm, LANES), F32), pltpu.VMEM((B_SLABS, tm, LANES), F32)],
        compiler_params=_params("parallel"),
        name="proj",
    )(x2, g1, w_ext, ones_bd, gqk, cosa, sina, cosb, sinb)


def _attn_a_kernel(qt_ref, k_ref, vt_ref, o_ref, w_ref, m_ref, acc_ref, s_ref, *, tq, tk, nk):
    nq = A_GROUP * tq
    w_ref[HEAD_DIM:, :] = jnp.zeros((LANES - HEAD_DIM, nq), BF16)
    for g in range(A_GROUP):
        w_ref[0:HEAD_DIM, g * tq:(g + 1) * tq] = qt_ref[0, g * HEAD_DIM:(g + 1) * HEAD_DIM, :]
    m_ref[...] = jnp.full(m_ref.shape, NEG_INF, F32)
    acc_ref[...] = jnp.zeros(acc_ref.shape, F32)

    def chunk_start(c):
        return c * tk if isinstance(c, int) else pl.multiple_of(c * tk, tk)

    def scores(c):
        k = k_ref[0, pl.ds(chunk_start(c), tk), :]
        return jnp.dot(k, w_ref[...], preferred_element_type=F32)

    def accumulate(s, c):
        vt = vt_ref[0, 0, :, pl.ds(chunk_start(c), tk)]
        s3 = s.reshape(tk // 8, 8, nq)
        m_prev = m_ref[...]
        m_new = jnp.maximum(m_prev, jnp.max(jnp.max(s3, axis=0), axis=0, keepdims=True))
        p = jnp.exp2(s3 - m_new[None]).reshape(tk, nq).astype(BF16)
        acc = acc_ref[...].reshape(VT_ROWS // 8, 8, nq) * jnp.exp2(m_prev - m_new)[None]
        acc_ref[...] = acc.reshape(VT_ROWS, nq) + jnp.dot(vt, p, preferred_element_type=F32)
        m_ref[...] = m_new

    def pair(c0, last):
        s1 = scores(c0 + 1)
        accumulate(s_ref[...], c0)
        if not last:
            s_ref[...] = scores(c0 + 2)
        accumulate(s1, c0 + 1)

    s_ref[...] = scores(0)

    def body(i, carry):
        pair(2 * i, False)
        return carry

    lax.fori_loop(0, nk // 2 - 1, body, 0)
    pair(nk - 2, True)
    _attn_a_store(acc_ref[...], o_ref, tq)


def _attn_a_shift_kernel(shift_ref, qt_ref, k_ref, vt_ref, o_ref, w_ref, *, tq, tk, nk):
    nq = A_GROUP * tq
    q_all = jnp.concatenate(
        [qt_ref[0, g * HEAD_DIM:(g + 1) * HEAD_DIM, :] for g in range(A_GROUP)], axis=1)
    row = lax.broadcasted_iota(jnp.int32, (16, nq), 0)
    shift_rows = jnp.where(row == 0, -shift_ref[0], 0.0).astype(BF16)
    w_ref[...] = jnp.concatenate(
        [q_all, shift_rows, jnp.zeros((LANES - HEAD_DIM - 16, nq), BF16)], axis=0)

    def scores(c):
        return jnp.dot(k_ref[0, c * tk:(c + 1) * tk, :], w_ref[...], preferred_element_type=F32)

    acc = jnp.zeros((VT_ROWS, nq), F32)
    s = scores(0)
    for c in range(nk):
        s_next = scores(c + 1) if c + 1 < nk else None
        p = jnp.exp2(s).astype(BF16)
        acc = acc + jnp.dot(vt_ref[0, 0, :, c * tk:(c + 1) * tk], p, preferred_element_type=F32)
        s = s_next
    _attn_a_store(acc, o_ref, tq)


def _attn_a_store(acc, o_ref, tq):
    ot = acc[0:HEAD_DIM] / acc[HEAD_DIM:HEAD_DIM + 1]
    for j in range(A_GROUP // 2):
        pair = jnp.concatenate([ot[:, (2 * j) * tq:(2 * j + 1) * tq],
                                ot[:, (2 * j + 1) * tq:(2 * j + 2) * tq]], axis=0)
        o_ref[0, :, j * LANES:(j + 1) * LANES] = pair.T.astype(o_ref.dtype)


def _attn_a_call(qat, ka, vat, shift):
    b, _, s = qat.shape
    tq, tk = ATTN_A_Q, ATTN_A_K
    group_w = A_GROUP * HEAD_DIM
    nq = A_GROUP * tq
    in_specs = [
        pl.BlockSpec((1, group_w, tq), lambda bi, kv, qi: (bi, kv, qi)),
        pl.BlockSpec((1, s, LANES), lambda bi, kv, qi: (bi, 0, kv)),
        pl.BlockSpec((1, 1, VT_ROWS, s), lambda bi, kv, qi: (bi, kv, 0, 0)),
    ]
    common = dict(
        grid=(b, A_KV_HEADS, s // tq),
        out_specs=pl.BlockSpec((1, tq, group_w), lambda bi, kv, qi: (bi, qi, kv)),
        out_shape=jax.ShapeDtypeStruct((b, s, A_WIDTH), BF16),
        compiler_params=_params("parallel", "parallel", "parallel"),
    )

    def fixed_shift():
        return pl.pallas_call(
            functools.partial(_attn_a_shift_kernel, tq=tq, tk=tk, nk=s // tk),
            in_specs=[pl.BlockSpec(memory_space=pltpu.SMEM)] + in_specs,
            scratch_shapes=[pltpu.VMEM((LANES, nq), BF16)],
            name="attn_a_shift", **common)(shift, qat, ka, vat)

    def running_max():
        return pl.pallas_call(
            functools.partial(_attn_a_kernel, tq=tq, tk=tk, nk=s // tk),
            in_specs=in_specs,
            scratch_shapes=[
                pltpu.VMEM((LANES, nq), BF16),
                pltpu.VMEM((8, nq), F32),
                pltpu.VMEM((VT_ROWS, nq), F32),
                pltpu.VMEM((tk, nq), F32),
            ],
            name="attn_a", **common)(qat, ka, vat)

    return lax.cond(shift[0] <= MAX_FIXED_SHIFT, fixed_shift, running_max)


def _attn_b_kernel(q_ref, *rest, seq):
    npat = len(DILATIONS)
    kd_refs, vd_refs = rest[:npat], rest[npat:2 * npat]
    o_ref, bias_ref, op_ref, mp_ref, dp_ref = rest[2 * npat:]
    rb = ATTN_B_ROWS
    kw = rb + 2 * WINDOW_RADIUS
    step = pl.program_id(2)
    lane = lax.broadcasted_iota(jnp.int32, (rb, LANES), 1)
    upper = lane >= HEAD_DIM
    blocks = ATTN_B_POS // rb

    @pl.when(step == 0)
    def _():
        rel = (lax.broadcasted_iota(jnp.int32, (rb, kw), 0)
               - lax.broadcasted_iota(jnp.int32, (rb, kw), 1))
        for case in range(3):
            ok = jnp.abs(rel + WINDOW_RADIUS * case) <= WINDOW_RADIUS
            bias_ref[case] = jnp.where(ok, 0.0, NEG_INF)

    def block_scores(pi, dil, u):
        sub_len = seq // dil
        rt, off = u // dil, u % dil
        j0 = step * (ATTN_B_POS // dil) + rt * rb
        ws = jnp.clip(j0 - WINDOW_RADIUS, 0, sub_len - kw)
        qrows = pl.ds(rt * rb * dil + off, rb, stride=dil)
        krows = (off, pl.ds(pl.multiple_of(ws, WINDOW_RADIUS), kw))
        q = q_ref[0, 0, qrows, :].astype(BF16)
        zero = jnp.zeros_like(q)
        qs = jnp.concatenate([jnp.where(upper, zero, q), jnp.where(upper, q, zero)], axis=0)
        bias = bias_ref[(j0 - ws) // WINDOW_RADIUS]
        s = lax.dot_general(qs, kd_refs[pi][0, 0, krows[0], krows[1], :], (((1,), (1,)), ((), ())),
                            preferred_element_type=F32)
        return s + jnp.concatenate([bias, bias], axis=0), qrows, krows

    def block_output(pi, s, qrows, krows):
        m = jnp.max(s, axis=-1, keepdims=True)
        p = jnp.exp2(s - m)
        den = jnp.broadcast_to(jnp.sum(p, axis=-1, keepdims=True), (HEADS_PER_SLAB * rb, LANES))
        m = jnp.broadcast_to(m, (HEADS_PER_SLAB * rb, LANES))
        o = jnp.dot(p.astype(BF16), vd_refs[pi][0, 0, krows[0], krows[1], :],
                    preferred_element_type=F32)
        op_ref[pi, qrows, :] = jnp.where(upper, o[rb:], o[:rb])
        mp_ref[pi, qrows, :] = jnp.where(upper, m[rb:], m[:rb])
        dp_ref[pi, qrows, :] = jnp.where(upper, den[rb:], den[:rb])

    for pi, dil in enumerate(DILATIONS):
        def trip(i, carry, pi=pi, dil=dil):
            scored = [block_scores(pi, dil, i * ATTN_B_UNROLL + t) for t in range(ATTN_B_UNROLL)]
            for s, qrows, krows in scored:
                block_output(pi, s, qrows, krows)
            return carry
        lax.fori_loop(0, blocks // ATTN_B_UNROLL, trip, 0)

    mmax = jnp.maximum(jnp.maximum(mp_ref[0], mp_ref[1]), mp_ref[2])
    num = jnp.zeros(mmax.shape, F32)
    den = jnp.zeros(mmax.shape, F32)
    for pi in range(len(DILATIONS)):
        e = jnp.exp2(mp_ref[pi] - mmax)
        num = num + e * op_ref[pi]
        den = den + e * dp_ref[pi]
    o_ref[0, 0] = (num / den).astype(o_ref.dtype)


def _attn_b_call(q4, kds, vds):
    nslab, b, s, _ = q4.shape
    tile = pl.BlockSpec((1, 1, ATTN_B_POS, LANES), lambda sl, bi, t: (sl, bi, t, 0))
    full = [pl.BlockSpec((1, 1, d, s // d, LANES), lambda sl, bi, t: (sl, bi, 0, 0, 0))
            for d in DILATIONS]
    npat = len(DILATIONS)
    kw = ATTN_B_ROWS + 2 * WINDOW_RADIUS
    return pl.pallas_call(
        functools.partial(_attn_b_kernel, seq=s),
        grid=(nslab, b, s // ATTN_B_POS),
        in_specs=[tile] + full + full,
        out_specs=tile,
        out_shape=jax.ShapeDtypeStruct((nslab, b, s, LANES), BF16),
        scratch_shapes=[pltpu.VMEM((3, ATTN_B_ROWS, kw), F32),
                        pltpu.VMEM((npat, ATTN_B_POS, LANES), F32),
                        pltpu.VMEM((npat, ATTN_B_POS, LANES), F32),
                        pltpu.VMEM((npat, ATTN_B_POS, LANES), F32)],
        compiler_params=_params("parallel", "parallel", "arbitrary"),
        name="attn_b",
    )(q4, *kds, *vds)


def _out_route_kernel(x_ref, oa_ref, ob_ref, ga_ref, gb_ref, wout_ref, g2_ref, wr_ref, rb_ref,
                      tri_ref, h_ref, route_ref, counts_ref, base_ref):
    tm = x_ref.shape[0]

    @pl.when(pl.program_id(0) == 0)
    def _():
        base_ref[...] = jnp.zeros(base_ref.shape, F32)

    oa = _rms(oa_ref[...].astype(F32), ga_ref[...])
    ob = jnp.concatenate([ob_ref[c].astype(F32) for c in range(B_SLABS)], axis=-1)
    ob = _rms(ob, gb_ref[...])
    cat = jnp.concatenate([oa, ob], axis=-1).astype(BF16)
    h = x_ref[...] + jnp.dot(cat, wout_ref[...], preferred_element_type=F32)
    h_ref[...] = h

    xt = _rms(h, g2_ref[...])
    xh = xt.astype(BF16)
    xl = (xt - xh.astype(F32)).astype(BF16)
    both = jnp.dot(xh, wr_ref[...], preferred_element_type=F32)
    logits = (both[:, :LANES] + both[:, LANES:]
              + jnp.dot(xl, wr_ref[:, :LANES], preferred_element_type=F32)) + rb_ref[...]

    lane = lax.broadcasted_iota(jnp.int32, (tm, LANES), 1).astype(F32)
    none = float(LANES)

    def first_argmax(vals):
        top = jnp.max(vals, axis=-1, keepdims=True)
        idx = jnp.min(jnp.where(vals == top, lane, none), axis=-1, keepdims=True)
        return top, idx

    gl = jnp.where(lane < N_GROUPS, logits, -jnp.inf)
    gmax, gidx = first_argmax(gl)
    gprob = 1.0 / jnp.sum(jnp.exp(gl - gmax), axis=-1, keepdims=True)
    lo = N_GROUPS + EXPERTS_PER_GROUP * gidx
    el = jnp.where((lane >= lo) & (lane < lo + EXPERTS_PER_GROUP), logits, -jnp.inf)
    v1, i1 = first_argmax(el)
    v2, i2 = first_argmax(jnp.where(lane == i1, -jnp.inf, el))
    t = jnp.exp(v2 - v1)
    gate1 = gprob / (1.0 + t)
    gate2 = gprob * t / (1.0 + t)
    x1 = i1 - N_GROUPS
    x2 = i2 - N_GROUPS

    hot1 = lane == x1
    hot2 = lane == x2
    hot = jnp.where(hot1 | hot2, 1.0, 0.0)
    before = base_ref[...] + jnp.dot(tri_ref[...], hot.astype(BF16), preferred_element_type=F32)
    r1 = jnp.sum(jnp.where(hot1, before, 0.0), axis=-1, keepdims=True)
    r2 = jnp.sum(jnp.where(hot2, before, 0.0), axis=-1, keepdims=True)
    base = base_ref[...] + jnp.sum(hot, axis=0, keepdims=True)
    base_ref[...] = base
    counts_ref[...] = jnp.broadcast_to(base, counts_ref.shape)

    packed = jnp.zeros((tm, LANES), F32)
    for i, col in enumerate((x1, x2, gate1, gate2, r1, r2)):
        packed = jnp.where(lane == i, col, packed)
    route_ref[...] = packed


def _out_route_call(x2, oa, ob, ga, gb, wout, g2, wr2, rbias, tri):
    n, dm = x2.shape
    tm = PROJ_ROWS
    const = lambda i: (0, 0)
    row = lambda i: (i, 0)
    slab = pl.BlockSpec((B_SLABS, tm, LANES), lambda i: (0, i, 0))
    return pl.pallas_call(
        _out_route_kernel,
        grid=(n // tm,),
        in_specs=[
            pl.BlockSpec((tm, dm), row),
            pl.BlockSpec((tm, A_WIDTH), row),
            slab,
            pl.BlockSpec((1, A_WIDTH), const),
            pl.BlockSpec((1, B_WIDTH), const),
            pl.BlockSpec((A_WIDTH + B_WIDTH, dm), const),
            pl.BlockSpec((1, dm), const),
            pl.BlockSpec((dm, 2 * LANES), const),
            pl.BlockSpec((1, LANES), const),
            pl.BlockSpec((tm, tm), const),
        ],
        out_specs=[
            pl.BlockSpec((tm, dm), row),
            pl.BlockSpec((tm, LANES), row),
            pl.BlockSpec((8, LANES), const),
        ],
        out_shape=[
            jax.ShapeDtypeStruct((n, dm), F32),
            jax.ShapeDtypeStruct((n, LANES), F32),
            jax.ShapeDtypeStruct((8, LANES), F32),
        ],
        scratch_shapes=[pltpu.VMEM((1, LANES), F32)],
        compiler_params=_params("arbitrary"),
        name="out_route",
    )(x2, oa, ob, ga, gb, wout, g2, wr2, rbias, tri)


def _tile_row(ref, group, sub):
    return ref.at[group, pl.ds(sub, 1)]


def _hbm_row(ref, row):
    return ref.at[pl.ds(row, 1)]


def _block_copy(src_ref, dst_ref, dst_row, sem):
    return pltpu.make_async_copy(src_ref, dst_ref.at[pl.ds(dst_row, MOE_BLOCK)], sem)


def _push_kernel(tail_ref, dest_ref, h_ref, g2_ref, xs_ref, xt_ref, zero_ref, sem):
    tm = h_ref.shape[0]

    @pl.when(pl.program_id(0) == 0)
    def _():
        zero_ref[...] = jnp.zeros(zero_ref.shape, F32)
        for e in range(tail_ref.shape[0]):
            @pl.when(tail_ref[e] >= 0)
            def _():
                _block_copy(zero_ref, xs_ref, pl.multiple_of(tail_ref[e], MOE_BLOCK), sem).start()
        for e in range(tail_ref.shape[0]):
            @pl.when(tail_ref[e] >= 0)
            def _():
                _block_copy(zero_ref, xs_ref, pl.multiple_of(tail_ref[e], MOE_BLOCK), sem).wait()

    xt_ref[...] = _rms(h_ref[...], g2_ref[...]).reshape(xt_ref.shape)

    def start(g, carry):
        for sub in range(SUBLANES):
            for c in range(2):
                dst = dest_ref[0, 0, 2 * SUBLANES * g + 2 * sub + c]
                pltpu.make_async_copy(_tile_row(xt_ref, g, sub), _hbm_row(xs_ref, dst), sem).start()
        return carry

    lax.fori_loop(0, tm // SUBLANES, start, 0)

    def wait(g, carry):
        for _ in range(2 * SUBLANES):
            pltpu.make_async_copy(_tile_row(xt_ref, 0, 0), _hbm_row(xs_ref, 0), sem).wait()
        return carry

    lax.fori_loop(0, tm // SUBLANES, wait, 0)


def _push_call(tail_rows, dest3, h, g2, rows):
    n, dm = h.shape
    tm = PROJ_ROWS
    return pl.pallas_call(
        _push_kernel,
        grid=(n // tm,),
        in_specs=[
            pl.BlockSpec(memory_space=pltpu.SMEM),
            pl.BlockSpec((1, 1, 2 * tm), lambda i: (i, 0, 0), memory_space=pltpu.SMEM),
            pl.BlockSpec((tm, dm), lambda i: (i, 0)),
            pl.BlockSpec((1, dm), lambda i: (0, 0)),
        ],
        out_specs=pl.BlockSpec(memory_space=pl.ANY),
        out_shape=jax.ShapeDtypeStruct((rows, dm), F32),
        scratch_shapes=[pltpu.VMEM((tm // SUBLANES, SUBLANES, dm), F32),
                        pltpu.VMEM((MOE_BLOCK, dm), F32),
                        pltpu.SemaphoreType.DMA(())],
        compiler_params=_params("arbitrary"),
        name="push",
    )(tail_rows, dest3, h, g2)


def _moe_kernel(be_ref, nv_ref, xs_ref, wg_ref, wu_ref, wd_ref, y_ref, wgb_ref, wub_ref, wdb_ref):
    i = pl.program_id(0)
    used = i < nv_ref[0]

    @pl.when(jnp.logical_not(used))
    def _():
        y_ref[...] = jnp.zeros(y_ref.shape, F32)

    @pl.when(used & ((i == 0) | (be_ref[i] != be_ref[jnp.maximum(i - 1, 0)])))
    def _():
        wgb_ref[...] = wg_ref[0].astype(BF16)
        wub_ref[...] = wu_ref[0].astype(BF16)
        wdb_ref[...] = wd_ref[0].astype(BF16)

    @pl.when(used)
    def _():
        xb = xs_ref[...].astype(BF16)
        a = jnp.dot(xb, wgb_ref[...], preferred_element_type=F32)
        u = jnp.dot(xb, wub_ref[...], preferred_element_type=F32)
        hdn = (a / (1.0 + jnp.exp(-a))) * u
        y_ref[...] = jnp.dot(hdn.astype(BF16), wdb_ref[...], preferred_element_type=F32)


def _moe_call(block_e, n_valid, xs, wg, wu, wd):
    rows, dm = xs.shape
    de = wg.shape[-1]
    blk = lambda i, be, nv: (jnp.minimum(i, nv[0] - 1), 0)
    wsel = lambda i, be, nv: (be[jnp.minimum(i, nv[0] - 1)], 0, 0)
    grid_spec = pltpu.PrefetchScalarGridSpec(
        num_scalar_prefetch=2,
        grid=(rows // MOE_BLOCK,),
        in_specs=[
            pl.BlockSpec((MOE_BLOCK, dm), blk),
            pl.BlockSpec((1, dm, de), wsel),
            pl.BlockSpec((1, dm, de), wsel),
            pl.BlockSpec((1, de, dm), wsel),
        ],
        out_specs=pl.BlockSpec((MOE_BLOCK, dm), lambda i, be, nv: (i, 0)),
        scratch_shapes=[pltpu.VMEM((dm, de), BF16), pltpu.VMEM((dm, de), BF16),
                        pltpu.VMEM((de, dm), BF16)],
    )
    return pl.pallas_call(
        _moe_kernel,
        grid_spec=grid_spec,
        out_shape=jax.ShapeDtypeStruct((rows, dm), F32),
        compiler_params=_params("arbitrary"),
        name="moe",
    )(block_e, n_valid, xs, wg, wu, wd)


def _final_kernel(dest_ref, h_ref, route_ref, gf_ref, yb_ref, out_ref, y1_ref, y2_ref, sem):
    tm = h_ref.shape[0]

    def start(g, carry):
        for sub in range(SUBLANES):
            for c, y_ref in enumerate((y1_ref, y2_ref)):
                src = dest_ref[0, 0, 2 * SUBLANES * g + 2 * sub + c]
                pltpu.make_async_copy(_hbm_row(yb_ref, src), _tile_row(y_ref, g, sub), sem).start()
        return carry

    lax.fori_loop(0, tm // SUBLANES, start, 0)

    def wait(g, carry):
        for _ in range(2 * SUBLANES):
            pltpu.make_async_copy(_hbm_row(yb_ref, 0), _tile_row(y1_ref, 0, 0), sem).wait()
        return carry

    lax.fori_loop(0, tm // SUBLANES, wait, 0)
    route = route_ref[...]
    dm = h_ref.shape[1]
    y = (route[:, 2:3] * y1_ref[...].reshape(tm, dm)
         + route[:, 3:4] * y2_ref[...].reshape(tm, dm))
    out_ref[...] = _rms(h_ref[...] + y, gf_ref[...])


def _final_call(dest3, h, route, gf, yb):
    n, dm = h.shape
    tm = PROJ_ROWS
    return pl.pallas_call(
        _final_kernel,
        grid=(n // tm,),
        in_specs=[
            pl.BlockSpec((1, 1, 2 * tm), lambda i: (i, 0, 0), memory_space=pltpu.SMEM),
            pl.BlockSpec((tm, dm), lambda i: (i, 0)),
            pl.BlockSpec((tm, LANES), lambda i: (i, 0)),
            pl.BlockSpec((1, dm), lambda i: (0, 0)),
            pl.BlockSpec(memory_space=pl.ANY),
        ],
        out_specs=pl.BlockSpec((tm, dm), lambda i: (i, 0)),
        out_shape=jax.ShapeDtypeStruct((n, dm), F32),
        scratch_shapes=[pltpu.VMEM((tm // SUBLANES, SUBLANES, dm), F32),
                        pltpu.VMEM((tm // SUBLANES, SUBLANES, dm), F32),
                        pltpu.SemaphoreType.DMA(())],
        compiler_params=_params("arbitrary"),
        name="final",
    )(dest3, h, route, gf, yb)


def _rope_inv_freq(dim):
    return 1.0 / (ROPE_THETA ** (jnp.arange(0, dim, 2, dtype=F32) / dim))


def _rotary_tables(angles):
    cos = jnp.tile(jnp.cos(angles), (1, LANES // angles.shape[1]))
    sin = jnp.sin(angles)
    sin = jnp.tile(jnp.concatenate([-sin, sin], axis=-1), (1, HEADS_PER_SLAB))
    return cos, sin


def _axial_angles(seq):
    rows = seq // GRID_W
    row = jnp.repeat(jnp.arange(rows, dtype=F32), GRID_W)
    col = jnp.tile(jnp.arange(GRID_W, dtype=F32), rows)
    f = _rope_inv_freq(HEAD_DIM // 2)
    return jnp.concatenate([row[:, None] * f, col[:, None] * f], axis=-1)


def _linear_angles(seq):
    return jnp.arange(seq, dtype=F32)[:, None] * _rope_inv_freq(HEAD_DIM)


def _extended_w_in(w):
    scale = HEAD_DIM ** -0.5
    o = 0
    qa = w[:, o:o + A_WIDTH]; o += A_WIDTH
    ka = w[:, o:o + A_KV_WIDTH]; o += A_KV_WIDTH
    va = w[:, o:o + A_KV_WIDTH]; o += A_KV_WIDTH
    qb = w[:, o:o + B_WIDTH]; o += B_WIDTH
    kb = w[:, o:o + B_WIDTH]; o += B_WIDTH
    vb = w[:, o:o + B_WIDTH]

    def dup(t):
        heads = [t[:, h * HEAD_DIM:(h + 1) * HEAD_DIM] for h in range(A_KV_HEADS)]
        return jnp.concatenate([p for h in heads for p in (h, h)], axis=1)

    return jnp.concatenate([qa, dup(ka), va, qb * (scale * LOG2_E), kb, vb], axis=1).astype(BF16)


def _layer(h2, batch, seq, p):
    n, dm = h2.shape
    scale = HEAD_DIM ** -0.5
    gqk = jnp.concatenate([jnp.tile(p["q_norm_g"] * (scale * LOG2_E), A_HEADS),
                           jnp.tile(p["k_norm_g"], 2 * A_KV_HEADS)])[None, :]
    gi = jnp.arange(MXU_TILE) // HEAD_DIM
    ones_bd = (gi[:, None] == gi[None, :]).astype(BF16)
    cosa, sina = _rotary_tables(_axial_angles(seq))
    cosb, sinb = _rotary_tables(_linear_angles(seq))

    qat, ka, vat, qb, *kvd = _proj_call(h2, p["norm1_g"][None, :], _extended_w_in(p["w_in"]),
                                        ones_bd, gqk, cosa, sina, cosb, sinb, seq)
    kds, vds = kvd[:len(DILATIONS)], kvd[len(DILATIONS):]

    shift = (HEAD_DIM * scale * LOG2_E * SHIFT_MARGIN
             * jnp.max(jnp.abs(p["q_norm_g"])) * jnp.max(jnp.abs(p["k_norm_g"])))
    shift = shift.astype(BF16).astype(F32)[None]
    oa = _attn_a_call(qat, ka.reshape(batch, seq, 2 * A_KV_WIDTH), vat, shift).reshape(n, A_WIDTH)

    ob = _attn_b_call(qb.reshape(B_SLABS, batch, seq, LANES), kds, vds).reshape(B_SLABS, n, LANES)

    wr = jnp.concatenate([p["router_group_w"], p["router_expert_w"]], axis=1)
    wr = jnp.pad(wr, ((0, 0), (0, LANES - wr.shape[1])))
    whi = wr.astype(BF16)
    wlo = (wr - whi.astype(F32)).astype(BF16)
    rbias = jnp.concatenate([p["router_group_b"], p["router_expert_b"]])
    rbias = jnp.pad(rbias, (0, LANES - rbias.shape[0]))[None, :]
    ti = jnp.arange(PROJ_ROWS)
    tri = (ti[:, None] > ti[None, :]).astype(BF16)

    h1, route, counts = _out_route_call(
        h2, oa, ob, p["out_norm_a_g"][None, :], p["out_norm_b_g"][None, :], p["w_out"].astype(BF16),
        p["norm2_g"][None, :], jnp.concatenate([whi, wlo], axis=1), rbias, tri)

    counts = counts[0, :N_EXPERTS].astype(jnp.int32)
    padded = (counts + MOE_BLOCK - 1) // MOE_BLOCK * MOE_BLOCK
    pend = jnp.cumsum(padded)
    pstart = pend - padded
    expert = route[:, 0:2].astype(jnp.int32)
    hot = expert[:, :, None] == jnp.arange(N_EXPERTS, dtype=jnp.int32)
    dest = jnp.sum(jnp.where(hot, pstart, 0), axis=-1) + route[:, 4:6].astype(jnp.int32)
    dest3 = dest.reshape(n // PROJ_ROWS, 1, 2 * PROJ_ROWS)
    n_blocks = -(-(2 * n) // MOE_BLOCK) + N_EXPERTS
    block_row = jnp.arange(n_blocks, dtype=jnp.int32) * MOE_BLOCK
    block_e = jnp.minimum(jnp.sum((block_row[:, None] >= pend[None, :]).astype(jnp.int32), axis=1),
                          N_EXPERTS - 1)
    n_valid = (pend[-1:] // MOE_BLOCK).astype(jnp.int32)
    unused = pend[-1] + jnp.arange(N_EXPERTS, dtype=jnp.int32) * MOE_BLOCK
    tail_rows = jnp.concatenate([jnp.where(padded > 0, pend - MOE_BLOCK, -1),
                                 jnp.where(unused < n_blocks * MOE_BLOCK, unused, -1)])
    tail_rows = tail_rows.astype(jnp.int32)

    xs = _push_call(tail_rows, dest3, h1, p["norm2_g"][None, :], n_blocks * MOE_BLOCK)
    yb = _moe_call(block_e, n_valid, xs, p["w_gate"], p["w_up"], p["w_down"])
    return dest3, h1, route, yb


def kernel(x, norm1_g, w_in, q_norm_g, k_norm_g, out_norm_a_g, out_norm_b_g, w_out, norm2_g,
           router_group_w, router_group_b, router_expert_w, router_expert_b, w_gate, w_up, w_down,
           final_norm_g):
    batch, seq, dm = x.shape
    assert dm == A_WIDTH + B_WIDTH and norm1_g.shape[0] == 1
    assert seq % max(ATTN_B_POS, PROJ_ROWS, ATTN_A_K) == 0
    assert seq // max(DILATIONS) >= ATTN_B_ROWS + 2 * WINDOW_RADIUS
    layer = dict(norm1_g=norm1_g[0], w_in=w_in[0], q_norm_g=q_norm_g[0], k_norm_g=k_norm_g[0],
                 out_norm_a_g=out_norm_a_g[0], out_norm_b_g=out_norm_b_g[0], w_out=w_out[0],
                 norm2_g=norm2_g[0], router_group_w=router_group_w[0],
                 router_group_b=router_group_b[0], router_expert_w=router_expert_w[0],
                 router_expert_b=router_expert_b[0], w_gate=w_gate[0], w_up=w_up[0],
                 w_down=w_down[0])
    dest3, h1, route, yb = _layer(x.reshape(batch * seq, dm), batch, seq, layer)
    out = _final_call(dest3, h1, route, final_norm_g[None, :], yb)
    return out.reshape(batch, seq, dm)
```

```python
import functools

import jax
import jax.numpy as jnp
from jax import lax
from jax.experimental import pallas as pl
from jax.experimental.pallas import tpu as pltpu

F32 = jnp.float32
BF16 = jnp.bfloat16

HEAD_DIM = 64
A_HEADS = 8
A_KV_HEADS = 2
A_GROUP = A_HEADS // A_KV_HEADS
B_HEADS = 8
A_WIDTH = A_HEADS * HEAD_DIM
A_KV_WIDTH = A_KV_HEADS * HEAD_DIM
B_WIDTH = B_HEADS * HEAD_DIM
ROPE_THETA = 10000.0
GRID_W = 64
DILATIONS = (1, 4, 16)
WINDOW_RADIUS = 64
N_GROUPS = 4
EXPERTS_PER_GROUP = 8
N_EXPERTS = N_GROUPS * EXPERTS_PER_GROUP
MOE_BLOCK = 512
EPS = 1e-6
NEG_INF = -1e30
LOG2_E = 1.4426950408889634
SHIFT_MARGIN = 1.02
MAX_FIXED_SHIFT = 50.0

LANES = 128
SUBLANES = 8
HEADS_PER_SLAB = LANES // HEAD_DIM
B_SLABS = B_WIDTH // LANES
VT_ROWS = HEAD_DIM + 16
VMEM_LIMIT_BYTES = 56 * 1024 * 1024

PROJ_ROWS = 512
ATTN_A_Q = 128
ATTN_A_K = 512
ATTN_B_ROWS = 128
ATTN_B_POS = 2048
ATTN_B_UNROLL = 8
MXU_TILE = 256

_QA = (0, A_WIDTH)
_KA = (_QA[1], _QA[1] + 2 * A_KV_WIDTH)
_VA = (_KA[1], _KA[1] + A_KV_WIDTH)
_QB = (_VA[1], _VA[1] + B_WIDTH)
_KB = (_QB[1], _QB[1] + B_WIDTH)
_VB = (_KB[1], _KB[1] + B_WIDTH)
EXT_WIDTH = _VB[1]
QK_A_WIDTH = _KA[1]


def _params(*semantics):
    return pltpu.CompilerParams(dimension_semantics=semantics,
                                vmem_limit_bytes=VMEM_LIMIT_BYTES)


def _rms(xf, g):
    return xf * lax.rsqrt(jnp.mean(xf * xf, axis=-1, keepdims=True) + EPS) * g


def _swap_half_heads(x):
    lane = lax.broadcasted_iota(jnp.int32, x.shape, 1)
    first = (lane % HEAD_DIM) < (HEAD_DIM // 2)
    return jnp.where(first, pltpu.roll(x, LANES - HEAD_DIM // 2, 1),
                     pltpu.roll(x, HEAD_DIM // 2, 1))


def _rotary(x, cos, sin_signed):
    return x * cos + _swap_half_heads(x) * sin_signed


def _proj_kernel(x_ref, g1_ref, w_ref, ones_ref, gqk_ref, cosa_ref, sina_ref, cosb_ref, sinb_ref,
                 qat_ref, ka_ref, vat_ref, qb_ref, *rest):
    npat = len(DILATIONS)
    kd_refs, vd_refs = rest[:npat], rest[npat:2 * npat]
    kst_ref, vst_ref = rest[2 * npat:]
    tm = x_ref.shape[0]
    u = _rms(x_ref[...], g1_ref[...]).astype(BF16)

    def proj(cols):
        return jnp.dot(u, w_ref[:, cols[0]:cols[1]], preferred_element_type=F32)

    a = proj((0, QK_A_WIDTH))
    sq = (a * a).astype(BF16)
    ss = jnp.concatenate(
        [jnp.dot(sq[:, j:j + MXU_TILE], ones_ref[...], preferred_element_type=F32)
         for j in range(0, QK_A_WIDTH, MXU_TILE)], axis=1)
    an = a * lax.rsqrt(ss * (1.0 / HEAD_DIM) + EPS) * gqk_ref[...]
    cosa, sina = cosa_ref[...], sina_ref[...]
    for c in range(QK_A_WIDTH // LANES):
        r = _rotary(an[:, c * LANES:(c + 1) * LANES], cosa, sina)
        if c < A_WIDTH // LANES:
            qat_ref[0, c * LANES:(c + 1) * LANES, :] = r.T.astype(BF16)
        else:
            c2 = c - A_WIDTH // LANES
            lane = lax.broadcasted_iota(jnp.int32, r.shape, 1)
            r = jnp.where(lane < HEAD_DIM, r, jnp.where(lane == HEAD_DIM, 1.0, 0.0))
            ka_ref[:, c2 * LANES:(c2 + 1) * LANES] = r.astype(BF16)
    vat = proj(_VA).T
    row = lax.broadcasted_iota(jnp.int32, (VT_ROWS - HEAD_DIM, tm), 0)
    ones_row = jnp.where(row == 0, 1.0, 0.0).astype(BF16)
    for kv in range(A_KV_HEADS):
        vat_ref[0, kv, 0:HEAD_DIM, :] = vat[kv * HEAD_DIM:(kv + 1) * HEAD_DIM].astype(BF16)
        vat_ref[0, kv, HEAD_DIM:VT_ROWS, :] = ones_row

    cosb, sinb = cosb_ref[...], sinb_ref[...]
    t = proj(_QB)
    for c in range(B_SLABS):
        qb_ref[c] = _rotary(t[:, c * LANES:(c + 1) * LANES], cosb, sinb)

    def store_regrouped(val, st_ref, outs, c):
        st_ref[c] = val
        for d, out_ref in zip(DILATIONS, outs):
            for off in range(d):
                rows = val if d == 1 else st_ref[c, pl.ds(off, tm // d, stride=d), :]
                out_ref[c, 0, off] = rows.astype(BF16)

    t = proj(_KB)
    for c in range(B_SLABS):
        store_regrouped(_rotary(t[:, c * LANES:(c + 1) * LANES], cosb, sinb), kst_ref, kd_refs, c)
    t = proj(_VB)
    for c in range(B_SLABS):
        store_regrouped(t[:, c * LANES:(c + 1) * LANES], vst_ref, vd_refs, c)


def _proj_call(x2, g1, w_ext, ones_bd, gqk, cosa, sina, cosb, sinb, seq):
    n, dm = x2.shape
    tm = PROJ_ROWS
    pos_blocks = seq // tm
    const = lambda i: (0, 0)
    row = lambda i: (i, 0)
    pos = lambda i: (i % pos_blocks, 0)
    slab = lambda i: (0, i, 0)
    batch = n // seq
    regrouped_specs = [
        pl.BlockSpec((B_SLABS, 1, d, tm // d, LANES),
                     lambda i: (0, i // pos_blocks, 0, i % pos_blocks, 0))
        for d in DILATIONS]
    regrouped_shapes = [jax.ShapeDtypeStruct((B_SLABS, batch, d, seq // d, LANES), BF16)
                        for d in DILATIONS]
    return pl.pallas_call(
        _proj_kernel,
        grid=(n // tm,),
        in_specs=[
            pl.BlockSpec((tm, dm), row),
            pl.BlockSpec((1, dm), const),
            pl.BlockSpec((dm, EXT_WIDTH), const),
            pl.BlockSpec((MXU_TILE, MXU_TILE), const),
            pl.BlockSpec((1, QK_A_WIDTH), const),
            pl.BlockSpec((tm, LANES), pos),
            pl.BlockSpec((tm, LANES), pos),
            pl.BlockSpec((tm, LANES), pos),
            pl.BlockSpec((tm, LANES), pos),
        ],
        out_specs=[
            pl.BlockSpec((1, A_WIDTH, tm), lambda i: (i // pos_blocks, 0, i % pos_blocks)),
            pl.BlockSpec((tm, 2 * A_KV_WIDTH), row),
            pl.BlockSpec((1, A_KV_HEADS, VT_ROWS, tm),
                         lambda i: (i // pos_blocks, 0, 0, i % pos_blocks)),
            pl.BlockSpec((B_SLABS, tm, LANES), slab),
        ] + regrouped_specs + regrouped_specs,
        out_shape=[
            jax.ShapeDtypeStruct((batch, A_WIDTH, seq), BF16),
            jax.ShapeDtypeStruct((n, 2 * A_KV_WIDTH), BF16),
            jax.ShapeDtypeStruct((batch, A_KV_HEADS, VT_ROWS, seq), BF16),
            jax.ShapeDtypeStruct((B_SLABS, n, LANES), F32),
        ] + regrouped_shapes + regrouped_shapes,
        scratch_shapes=[pltpu.VMEM((B_SLABS, tm, LANES), F32), pltpu.VMEM((B_SLABS, tm, LANES), F32)],
        compiler_params=_params("parallel"),
        name="proj",
    )(x2, g1, w_ext, ones_bd, gqk, cosa, sina, cosb, sinb)


def _attn_a_kernel(qt_ref, k_ref, vt_ref, o_ref, w_ref, m_ref, acc_ref, s_ref, *, tq, tk, nk):
    nq = A_GROUP * tq
    w_ref[HEAD_DIM:, :] = jnp.zeros((LANES - HEAD_DIM, nq), BF16)
    for g in range(A_GROUP):
        w_ref[0:HEAD_DIM, g * tq:(g + 1) * tq] = qt_ref[0, g * HEAD_DIM:(g + 1) * HEAD_DIM, :]
    m_ref[...] = jnp.full(m_ref.shape, NEG_INF, F32)
    acc_ref[...] = jnp.zeros(acc_ref.shape, F32)

    def chunk_start(c):
        return c * tk if isinstance(c, int) else pl.multiple_of(c * tk, tk)

    def scores(c):
        k = k_ref[0, pl.ds(chunk_start(c), tk), :]
        return jnp.dot(k, w_ref[...], preferred_element_type=F32)

    def accumulate(s, c):
        vt = vt_ref[0, 0, :, pl.ds(chunk_start(c), tk)]
        s3 = s.reshape(tk // 8, 8, nq)
        m_prev = m_ref[...]
        m_new = jnp.maximum(m_prev, jnp.max(jnp.max(s3, axis=0), axis=0, keepdims=True))
        p = jnp.exp2(s3 - m_new[None]).reshape(tk, nq).astype(BF16)
        acc = acc_ref[...].reshape(VT_ROWS // 8, 8, nq) * jnp.exp2(m_prev - m_new)[None]
        acc_ref[...] = acc.reshape(VT_ROWS, nq) + jnp.dot(vt, p, preferred_element_type=F32)
        m_ref[...] = m_new

    def pair(c0, last):
        s1 = scores(c0 + 1)
        accumulate(s_ref[...], c0)
        if not last:
            s_ref[...] = scores(c0 + 2)
        accumulate(s1, c0 + 1)

    s_ref[...] = scores(0)

    def body(i, carry):
        pair(2 * i, False)
        return carry

    lax.fori_loop(0, nk // 2 - 1, body, 0)
    pair(nk - 2, True)
    _attn_a_store(acc_ref[...], o_ref, tq)


def _attn_a_shift_kernel(shift_ref, qt_ref, k_ref, vt_ref, o_ref, w_ref, *, tq, tk, nk):
    nq = A_GROUP * tq
    q_all = jnp.concatenate(
        [qt_ref[0, g * HEAD_DIM:(g + 1) * HEAD_DIM, :] for g in range(A_GROUP)], axis=1)
    row = lax.broadcasted_iota(jnp.int32, (16, nq), 0)
    shift_rows = jnp.where(row == 0, -shift_ref[0], 0.0).astype(BF16)
    w_ref[...] = jnp.concatenate(
        [q_all, shift_rows, jnp.zeros((LANES - HEAD_DIM - 16, nq), BF16)], axis=0)

    def scores(c):
        return jnp.dot(k_ref[0, c * tk:(c + 1) * tk, :], w_ref[...], preferred_element_type=F32)

    acc = jnp.zeros((HEAD_DIM, nq), F32)
    den = jnp.zeros((SUBLANES, nq), F32)
    s = scores(0)
    for c in range(nk):
        s_next = scores(c + 1) if c + 1 < nk else None
        p = jnp.exp2(s)
        den = den + jnp.sum(p.reshape(tk // SUBLANES, SUBLANES, nq), axis=0)
        acc = acc + jnp.dot(vt_ref[0, 0, 0:HEAD_DIM, c * tk:(c + 1) * tk], p.astype(BF16),
                            preferred_element_type=F32)
        s = s_next
    den = jnp.sum(den, axis=0, keepdims=True)
    _attn_a_store(jnp.concatenate([acc, jnp.broadcast_to(den, (SUBLANES, nq))], axis=0), o_ref, tq)


def _attn_a_store(acc, o_ref, tq):
    ot = acc[0:HEAD_DIM] / acc[HEAD_DIM:HEAD_DIM + 1]
    for j in range(A_GROUP // 2):
        pair = jnp.concatenate([ot[:, (2 * j) * tq:(2 * j + 1) * tq],
                                ot[:, (2 * j + 1) * tq:(2 * j + 2) * tq]], axis=0)
        o_ref[0, :, j * LANES:(j + 1) * LANES] = pair.T.astype(o_ref.dtype)


def _attn_a_call(qat, ka, vat, shift):
    b, _, s = qat.shape
    tq, tk = ATTN_A_Q, ATTN_A_K
    group_w = A_GROUP * HEAD_DIM
    nq = A_GROUP * tq
    in_specs = [
        pl.BlockSpec((1, group_w, tq), lambda bi, kv, qi: (bi, kv, qi)),
        pl.BlockSpec((1, s, LANES), lambda bi, kv, qi: (bi, 0, kv)),
        pl.BlockSpec((1, 1, VT_ROWS, s), lambda bi, kv, qi: (bi, kv, 0, 0)),
    ]
    common = dict(
        grid=(b, A_KV_HEADS, s // tq),
        out_specs=pl.BlockSpec((1, tq, group_w), lambda bi, kv, qi: (bi, qi, kv)),
        out_shape=jax.ShapeDtypeStruct((b, s, A_WIDTH), BF16),
        compiler_params=_params("parallel", "parallel", "parallel"),
    )

    def fixed_shift():
        return pl.pallas_call(
            functools.partial(_attn_a_shift_kernel, tq=tq, tk=tk, nk=s // tk),
            in_specs=[pl.BlockSpec(memory_space=pltpu.SMEM)] + in_specs,
            scratch_shapes=[pltpu.VMEM((LANES, nq), BF16)],
            name="attn_a_shift", **common)(shift, qat, ka, vat)

    def running_max():
        return pl.pallas_call(
            functools.partial(_attn_a_kernel, tq=tq, tk=tk, nk=s // tk),
            in_specs=in_specs,
            scratch_shapes=[
                pltpu.VMEM((LANES, nq), BF16),
                pltpu.VMEM((8, nq), F32),
                pltpu.VMEM((VT_ROWS, nq), F32),
                pltpu.VMEM((tk, nq), F32),
            ],
            name="attn_a", **common)(qat, ka, vat)

    return lax.cond(shift[0] <= MAX_FIXED_SHIFT, fixed_shift, running_max)


def _attn_b_kernel(q_ref, *rest, seq):
    npat = len(DILATIONS)
    kd_refs, vd_refs = rest[:npat], rest[npat:2 * npat]
    o_ref, bias_ref, op_ref, mp_ref, dp_ref = rest[2 * npat:]
    rb = ATTN_B_ROWS
    kw = rb + 2 * WINDOW_RADIUS
    step = pl.program_id(2)
    lane = lax.broadcasted_iota(jnp.int32, (rb, LANES), 1)
    upper = lane >= HEAD_DIM
    blocks = ATTN_B_POS // rb

    @pl.when(step == 0)
    def _():
        rel = (lax.broadcasted_iota(jnp.int32, (rb, kw), 0)
               - lax.broadcasted_iota(jnp.int32, (rb, kw), 1))
        for case in range(3):
            ok = jnp.abs(rel + WINDOW_RADIUS * case) <= WINDOW_RADIUS
            bias_ref[case] = jnp.where(ok, 0.0, NEG_INF)

    def block_scores(pi, dil, u):
        sub_len = seq // dil
        rt, off = u // dil, u % dil
        j0 = step * (ATTN_B_POS // dil) + rt * rb
        ws = jnp.clip(j0 - WINDOW_RADIUS, 0, sub_len - kw)
        qrows = pl.ds(rt * rb * dil + off, rb, stride=dil)
        krows = (off, pl.ds(pl.multiple_of(ws, WINDOW_RADIUS), kw))
        q = q_ref[0, 0, qrows, :].astype(BF16)
        zero = jnp.zeros_like(q)
        qs = jnp.concatenate([jnp.where(upper, zero, q), jnp.where(upper, q, zero)], axis=0)
        bias = bias_ref[(j0 - ws) // WINDOW_RADIUS]
        s = lax.dot_general(qs, kd_refs[pi][0, 0, krows[0], krows[1], :], (((1,), (1,)), ((), ())),
                            preferred_element_type=F32)
        return s + jnp.concatenate([bias, bias], axis=0), qrows, krows

    def block_output(pi, s, qrows, krows):
        m = jnp.max(s, axis=-1, keepdims=True)
        p = jnp.exp2(s - m)
        den = jnp.broadcast_to(jnp.sum(p, axis=-1, keepdims=True), (HEADS_PER_SLAB * rb, LANES))
        m = jnp.broadcast_to(m, (HEADS_PER_SLAB * rb, LANES))
        o = jnp.dot(p.astype(BF16), vd_refs[pi][0, 0, krows[0], krows[1], :],
                    preferred_element_type=F32)
        op_ref[pi, qrows, :] = jnp.where(upper, o[rb:], o[:rb])
        mp_ref[pi, qrows, :] = jnp.where(upper, m[rb:], m[:rb])
        dp_ref[pi, qrows, :] = jnp.where(upper, den[rb:], den[:rb])

    for pi, dil in enumerate(DILATIONS):
        def trip(i, carry, pi=pi, dil=dil):
            scored = [block_scores(pi, dil, i * ATTN_B_UNROLL + t) for t in range(ATTN_B_UNROLL)]
            for s, qrows, krows in scored:
                block_output(pi, s, qrows, krows)
            return carry
        lax.fori_loop(0, blocks // ATTN_B_UNROLL, trip, 0)

    mmax = jnp.maximum(jnp.maximum(mp_ref[0], mp_ref[1]), mp_ref[2])
    num = jnp.zeros(mmax.shape, F32)
    den = jnp.zeros(mmax.shape, F32)
    for pi in range(len(DILATIONS)):
        e = jnp.exp2(mp_ref[pi] - mmax)
        num = num + e * op_ref[pi]
        den = den + e * dp_ref[pi]
    o_ref[0, 0] = (num / den).astype(o_ref.dtype)


def _attn_b_call(q4, kds, vds):
    nslab, b, s, _ = q4.shape
    tile = pl.BlockSpec((1, 1, ATTN_B_POS, LANES), lambda sl, bi, t: (sl, bi, t, 0))
    full = [pl.BlockSpec((1, 1, d, s // d, LANES), lambda sl, bi, t: (sl, bi, 0, 0, 0))
            for d in DILATIONS]
    npat = len(DILATIONS)
    kw = ATTN_B_ROWS + 2 * WINDOW_RADIUS
    return pl.pallas_call(
        functools.partial(_attn_b_kernel, seq=s),
        grid=(nslab, b, s // ATTN_B_POS),
        in_specs=[tile] + full + full,
        out_specs=tile,
        out_shape=jax.ShapeDtypeStruct((nslab, b, s, LANES), BF16),
        scratch_shapes=[pltpu.VMEM((3, ATTN_B_ROWS, kw), F32),
                        pltpu.VMEM((npat, ATTN_B_POS, LANES), F32),
                        pltpu.VMEM((npat, ATTN_B_POS, LANES), F32),
                        pltpu.VMEM((npat, ATTN_B_POS, LANES), F32)],
        compiler_params=_params("parallel", "parallel", "arbitrary"),
        name="attn_b",
    )(q4, *kds, *vds)


def _out_route_kernel(x_ref, oa_ref, ob_ref, ga_ref, gb_ref, wout_ref, g2_ref, wr_ref, rb_ref,
                      tri_ref, h_ref, route_ref, counts_ref, base_ref):
    tm = x_ref.shape[0]

    @pl.when(pl.program_id(0) == 0)
    def _():
        base_ref[...] = jnp.zeros(base_ref.shape, F32)

    oa = _rms(oa_ref[...].astype(F32), ga_ref[...])
    ob = jnp.concatenate([ob_ref[c].astype(F32) for c in range(B_SLABS)], axis=-1)
    ob = _rms(ob, gb_ref[...])
    cat = jnp.concatenate([oa, ob], axis=-1).astype(BF16)
    h = x_ref[...] + jnp.dot(cat, wout_ref[...], preferred_element_type=F32)
    h_ref[...] = h

    xt = _rms(h, g2_ref[...])
    xh = xt.astype(BF16)
    xl = (xt - xh.astype(F32)).astype(BF16)
    both = jnp.dot(xh, wr_ref[...], preferred_element_type=F32)
    logits = (both[:, :LANES] + both[:, LANES:]
              + jnp.dot(xl, wr_ref[:, :LANES], preferred_element_type=F32)) + rb_ref[...]

    lane = lax.broadcasted_iota(jnp.int32, (tm, LANES), 1).astype(F32)
    none = float(LANES)

    def first_argmax(vals):
        top = jnp.max(vals, axis=-1, keepdims=True)
        idx = jnp.min(jnp.where(vals == top, lane, none), axis=-1, keepdims=True)
        return top, idx

    gl = jnp.where(lane < N_GROUPS, logits, -jnp.inf)
    gmax, gidx = first_argmax(gl)
    gprob = 1.0 / jnp.sum(jnp.exp(gl - gmax), axis=-1, keepdims=True)
    lo = N_GROUPS + EXPERTS_PER_GROUP * gidx
    el = jnp.where((lane >= lo) & (lane < lo + EXPERTS_PER_GROUP), logits, -jnp.inf)
    v1, i1 = first_argmax(el)
    v2, i2 = first_argmax(jnp.where(lane == i1, -jnp.inf, el))
    t = jnp.exp(v2 - v1)
    gate1 = gprob / (1.0 + t)
    gate2 = gprob * t / (1.0 + t)
    x1 = i1 - N_GROUPS
    x2 = i2 - N_GROUPS

    hot1 = lane == x1
    hot2 = lane == x2
    hot = jnp.where(hot1 | hot2, 1.0, 0.0)
    before = base_ref[...] + jnp.dot(tri_ref[...], hot.astype(BF16), preferred_element_type=F32)
    r1 = jnp.sum(jnp.where(hot1, before, 0.0), axis=-1, keepdims=True)
    r2 = jnp.sum(jnp.where(hot2, before, 0.0), axis=-1, keepdims=True)
    base = base_ref[...] + jnp.sum(hot, axis=0, keepdims=True)
    base_ref[...] = base
    counts_ref[...] = jnp.broadcast_to(base, counts_ref.shape)

    packed = jnp.zeros((tm, LANES), F32)
    for i, col in enumerate((x1, x2, gate1, gate2, r1, r2)):
        packed = jnp.where(lane == i, col, packed)
    route_ref[...] = packed


def _out_route_call(x2, oa, ob, ga, gb, wout, g2, wr2, rbias, tri):
    n, dm = x2.shape
    tm = PROJ_ROWS
    const = lambda i: (0, 0)
    row = lambda i: (i, 0)
    slab = pl.BlockSpec((B_SLABS, tm, LANES), lambda i: (0, i, 0))
    return pl.pallas_call(
        _out_route_kernel,
        grid=(n // tm,),
        in_specs=[
            pl.BlockSpec((tm, dm), row),
            pl.BlockSpec((tm, A_WIDTH), row),
            slab,
            pl.BlockSpec((1, A_WIDTH), const),
            pl.BlockSpec((1, B_WIDTH), const),
            pl.BlockSpec((A_WIDTH + B_WIDTH, dm), const),
            pl.BlockSpec((1, dm), const),
            pl.BlockSpec((dm, 2 * LANES), const),
            pl.BlockSpec((1, LANES), const),
            pl.BlockSpec((tm, tm), const),
        ],
        out_specs=[
            pl.BlockSpec((tm, dm), row),
            pl.BlockSpec((tm, LANES), row),
            pl.BlockSpec((8, LANES), const),
        ],
        out_shape=[
            jax.ShapeDtypeStruct((n, dm), F32),
            jax.ShapeDtypeStruct((n, LANES), F32),
            jax.ShapeDtypeStruct((8, LANES), F32),
        ],
        scratch_shapes=[pltpu.VMEM((1, LANES), F32)],
        compiler_params=_params("arbitrary"),
        name="out_route",
    )(x2, oa, ob, ga, gb, wout, g2, wr2, rbias, tri)


def _tile_row(ref, group, sub):
    return ref.at[group, pl.ds(sub, 1)]


def _hbm_row(ref, row):
    return ref.at[pl.ds(row, 1)]


def _block_copy(src_ref, dst_ref, dst_row, sem):
    return pltpu.make_async_copy(src_ref, dst_ref.at[pl.ds(dst_row, MOE_BLOCK)], sem)


def _push_kernel(tail_ref, dest_ref, h_ref, g2_ref, xs_ref, xt_ref, zero_ref, sem):
    tm = h_ref.shape[0]

    @pl.when(pl.program_id(0) == 0)
    def _():
        zero_ref[...] = jnp.zeros(zero_ref.shape, F32)
        for e in range(tail_ref.shape[0]):
            @pl.when(tail_ref[e] >= 0)
            def _():
                _block_copy(zero_ref, xs_ref, pl.multiple_of(tail_ref[e], MOE_BLOCK), sem).start()
        for e in range(tail_ref.shape[0]):
            @pl.when(tail_ref[e] >= 0)
            def _():
                _block_copy(zero_ref, xs_ref, pl.multiple_of(tail_ref[e], MOE_BLOCK), sem).wait()

    xt_ref[...] = _rms(h_ref[...], g2_ref[...]).reshape(xt_ref.shape)

    def start(g, carry):
        for sub in range(SUBLANES):
            for c in range(2):
                dst = dest_ref[0, 0, 2 * SUBLANES * g + 2 * sub + c]
                pltpu.make_async_copy(_tile_row(xt_ref, g, sub), _hbm_row(xs_ref, dst),
                                      sem).start(priority=c)
        return carry

    lax.fori_loop(0, tm // SUBLANES, start, 0)

    def wait(g, carry):
        for _ in range(2 * SUBLANES):
            pltpu.make_async_copy(_tile_row(xt_ref, 0, 0), _hbm_row(xs_ref, 0), sem).wait()
        return carry

    lax.fori_loop(0, tm // SUBLANES, wait, 0)


def _push_call(tail_rows, dest3, h, g2, rows):
    n, dm = h.shape
    tm = PROJ_ROWS
    return pl.pallas_call(
        _push_kernel,
        grid=(n // tm,),
        in_specs=[
            pl.BlockSpec(memory_space=pltpu.SMEM),
            pl.BlockSpec((1, 1, 2 * tm), lambda i: (i, 0, 0), memory_space=pltpu.SMEM),
            pl.BlockSpec((tm, dm), lambda i: (i, 0)),
            pl.BlockSpec((1, dm), lambda i: (0, 0)),
        ],
        out_specs=pl.BlockSpec(memory_space=pl.ANY),
        out_shape=jax.ShapeDtypeStruct((rows, dm), F32),
        scratch_shapes=[pltpu.VMEM((tm // SUBLANES, SUBLANES, dm), F32),
                        pltpu.VMEM((MOE_BLOCK, dm), F32),
                        pltpu.SemaphoreType.DMA(())],
        compiler_params=_params("arbitrary"),
        name="push",
    )(tail_rows, dest3, h, g2)


def _moe_kernel(be_ref, nv_ref, xs_ref, wg_ref, wu_ref, wd_ref, y_ref, wgb_ref, wub_ref, wdb_ref):
    i = pl.program_id(0)
    used = i < nv_ref[0]

    @pl.when(jnp.logical_not(used))
    def _():
        y_ref[...] = jnp.zeros(y_ref.shape, F32)

    @pl.when(used & ((i == 0) | (be_ref[i] != be_ref[jnp.maximum(i - 1, 0)])))
    def _():
        wgb_ref[...] = wg_ref[0].astype(BF16)
        wub_ref[...] = wu_ref[0].astype(BF16)
        wdb_ref[...] = wd_ref[0].astype(BF16)

    @pl.when(used)
    def _():
        xb = xs_ref[...].astype(BF16)
        a = jnp.dot(xb, wgb_ref[...], preferred_element_type=F32)
        u = jnp.dot(xb, wub_ref[...], preferred_element_type=F32)
        hdn = (a / (1.0 + jnp.exp(-a))) * u
        y_ref[...] = jnp.dot(hdn.astype(BF16), wdb_ref[...], preferred_element_type=F32)


def _moe_call(block_e, n_valid, xs, wg, wu, wd):
    rows, dm = xs.shape
    de = wg.shape[-1]
    blk = lambda i, be, nv: (jnp.minimum(i, nv[0] - 1), 0)
    wsel = lambda i, be, nv: (be[jnp.minimum(i, nv[0] - 1)], 0, 0)
    grid_spec = pltpu.PrefetchScalarGridSpec(
        num_scalar_prefetch=2,
        grid=(rows // MOE_BLOCK,),
        in_specs=[
            pl.BlockSpec((MOE_BLOCK, dm), blk),
            pl.BlockSpec((1, dm, de), wsel),
            pl.BlockSpec((1, dm, de), wsel),
            pl.BlockSpec((1, de, dm), wsel),
        ],
        out_specs=pl.BlockSpec((MOE_BLOCK, dm), lambda i, be, nv: (i, 0)),
        scratch_shapes=[pltpu.VMEM((dm, de), BF16), pltpu.VMEM((dm, de), BF16),
                        pltpu.VMEM((de, dm), BF16)],
    )
    return pl.pallas_call(
        _moe_kernel,
        grid_spec=grid_spec,
        out_shape=jax.ShapeDtypeStruct((rows, dm), F32),
        compiler_params=_params("arbitrary"),
        name="moe",
    )(block_e, n_valid, xs, wg, wu, wd)


def _final_kernel(dest_ref, h_ref, route_ref, gf_ref, yb_ref, out_ref, y1_ref, y2_ref, sem):
    tm = h_ref.shape[0]

    def start(g, carry):
        for sub in range(SUBLANES):
            for c, y_ref in enumerate((y1_ref, y2_ref)):
                src = dest_ref[0, 0, 2 * SUBLANES * g + 2 * sub + c]
                pltpu.make_async_copy(_hbm_row(yb_ref, src), _tile_row(y_ref, g, sub),
                                      sem).start(priority=c)
        return carry

    lax.fori_loop(0, tm // SUBLANES, start, 0)

    def wait(g, carry):
        for _ in range(2 * SUBLANES):
            pltpu.make_async_copy(_hbm_row(yb_ref, 0), _tile_row(y1_ref, 0, 0), sem).wait()
        return carry

    lax.fori_loop(0, tm // SUBLANES, wait, 0)
    route = route_ref[...]
    dm = h_ref.shape[1]
    y = (route[:, 2:3] * y1_ref[...].reshape(tm, dm)
         + route[:, 3:4] * y2_ref[...].reshape(tm, dm))
    out_ref[...] = _rms(h_ref[...] + y, gf_ref[...])


def _final_call(dest3, h, route, gf, yb):
    n, dm = h.shape
    tm = PROJ_ROWS
    return pl.pallas_call(
        _final_kernel,
        grid=(n // tm,),
        in_specs=[
            pl.BlockSpec((1, 1, 2 * tm), lambda i: (i, 0, 0), memory_space=pltpu.SMEM),
            pl.BlockSpec((tm, dm), lambda i: (i, 0)),
            pl.BlockSpec((tm, LANES), lambda i: (i, 0)),
            pl.BlockSpec((1, dm), lambda i: (0, 0)),
            pl.BlockSpec(memory_space=pl.ANY),
        ],
        out_specs=pl.BlockSpec((tm, dm), lambda i: (i, 0)),
        out_shape=jax.ShapeDtypeStruct((n, dm), F32),
        scratch_shapes=[pltpu.VMEM((tm // SUBLANES, SUBLANES, dm), F32),
                        pltpu.VMEM((tm // SUBLANES, SUBLANES, dm), F32),
                        pltpu.SemaphoreType.DMA(())],
        compiler_params=_params("arbitrary"),
        name="final",
    )(dest3, h, route, gf, yb)


def _rope_inv_freq(dim):
    return 1.0 / (ROPE_THETA ** (jnp.arange(0, dim, 2, dtype=F32) / dim))


def _rotary_tables(angles):
    cos = jnp.tile(jnp.cos(angles), (1, LANES // angles.shape[1]))
    sin = jnp.sin(angles)
    sin = jnp.tile(jnp.concatenate([-sin, sin], axis=-1), (1, HEADS_PER_SLAB))
    return cos, sin


def _axial_angles(seq):
    rows = seq // GRID_W
    row = jnp.repeat(jnp.arange(rows, dtype=F32), GRID_W)
    col = jnp.tile(jnp.arange(GRID_W, dtype=F32), rows)
    f = _rope_inv_freq(HEAD_DIM // 2)
    return jnp.concatenate([row[:, None] * f, col[:, None] * f], axis=-1)


def _linear_angles(seq):
    return jnp.arange(seq, dtype=F32)[:, None] * _rope_inv_freq(HEAD_DIM)


def _extended_w_in(w):
    scale = HEAD_DIM ** -0.5
    o = 0
    qa = w[:, o:o + A_WIDTH]; o += A_WIDTH
    ka = w[:, o:o + A_KV_WIDTH]; o += A_KV_WIDTH
    va = w[:, o:o + A_KV_WIDTH]; o += A_KV_WIDTH
    qb = w[:, o:o + B_WIDTH]; o += B_WIDTH
    kb = w[:, o:o + B_WIDTH]; o += B_WIDTH
    vb = w[:, o:o + B_WIDTH]

    def dup(t):
        heads = [t[:, h * HEAD_DIM:(h + 1) * HEAD_DIM] for h in range(A_KV_HEADS)]
        return jnp.concatenate([p for h in heads for p in (h, h)], axis=1)

    return jnp.concatenate([qa, dup(ka), va, qb * (scale * LOG2_E), kb, vb], axis=1).astype(BF16)


def _layer(h2, batch, seq, p):
    n, dm = h2.shape
    scale = HEAD_DIM ** -0.5
    gqk = jnp.concatenate([jnp.tile(p["q_norm_g"] * (scale * LOG2_E), A_HEADS),
                           jnp.tile(p["k_norm_g"], 2 * A_KV_HEADS)])[None, :]
    gi = jnp.arange(MXU_TILE) // HEAD_DIM
    ones_bd = (gi[:, None] == gi[None, :]).astype(BF16)
    cosa, sina = _rotary_tables(_axial_angles(seq))
    cosb, sinb = _rotary_tables(_linear_angles(seq))

    qat, ka, vat, qb, *kvd = _proj_call(h2, p["norm1_g"][None, :], _extended_w_in(p["w_in"]),
                                        ones_bd, gqk, cosa, sina, cosb, sinb, seq)
    kds, vds = kvd[:len(DILATIONS)], kvd[len(DILATIONS):]

    shift = (HEAD_DIM * scale * LOG2_E * SHIFT_MARGIN
             * jnp.max(jnp.abs(p["q_norm_g"])) * jnp.max(jnp.abs(p["k_norm_g"])))
    shift = shift.astype(BF16).astype(F32)[None]
    oa = _attn_a_call(qat, ka.reshape(batch, seq, 2 * A_KV_WIDTH), vat, shift).reshape(n, A_WIDTH)

    ob = _attn_b_call(qb.reshape(B_SLABS, batch, seq, LANES), kds, vds).reshape(B_SLABS, n, LANES)

    wr = jnp.concatenate([p["router_group_w"], p["router_expert_w"]], axis=1)
    wr = jnp.pad(wr, ((0, 0), (0, LANES - wr.shape[1])))
    whi = wr.astype(BF16)
    wlo = (wr - whi.astype(F32)).astype(BF16)
    rbias = jnp.concatenate([p["router_group_b"], p["router_expert_b"]])
    rbias = jnp.pad(rbias, (0, LANES - rbias.shape[0]))[None, :]
    ti = jnp.arange(PROJ_ROWS)
    tri = (ti[:, None] > ti[None, :]).astype(BF16)

    h1, route, counts = _out_route_call(
        h2, oa, ob, p["out_norm_a_g"][None, :], p["out_norm_b_g"][None, :], p["w_out"].astype(BF16),
        p["norm2_g"][None, :], jnp.concatenate([whi, wlo], axis=1), rbias, tri)

    counts = counts[0, :N_EXPERTS].astype(jnp.int32)
    padded = (counts + MOE_BLOCK - 1) // MOE_BLOCK * MOE_BLOCK
    pend = jnp.cumsum(padded)
    pstart = pend - padded
    expert = route[:, 0:2].astype(jnp.int32)
    hot = expert[:, :, None] == jnp.arange(N_EXPERTS, dtype=jnp.int32)
    dest = jnp.sum(jnp.where(hot, pstart, 0), axis=-1) + route[:, 4:6].astype(jnp.int32)
    dest3 = dest.reshape(n // PROJ_ROWS, 1, 2 * PROJ_ROWS)
    n_blocks = -(-(2 * n) // MOE_BLOCK) + N_EXPERTS
    block_row = jnp.arange(n_blocks, dtype=jnp.int32) * MOE_BLOCK
    block_e = jnp.minimum(jnp.sum((block_row[:, None] >= pend[None, :]).astype(jnp.int32), axis=1),
                          N_EXPERTS - 1)
    n_valid = (pend[-1:] // MOE_BLOCK).astype(jnp.int32)
    unused = pend[-1] + jnp.arange(N_EXPERTS, dtype=jnp.int32) * MOE_BLOCK
    tail_rows = jnp.concatenate([jnp.where(padded > 0, pend - MOE_BLOCK, -1),
                                 jnp.where(unused < n_blocks * MOE_BLOCK, unused, -1)])
    tail_rows = tail_rows.astype(jnp.int32)

    xs = _push_call(tail_rows, dest3, h1, p["norm2_g"][None, :], n_blocks * MOE_BLOCK)
    yb = _moe_call(block_e, n_valid, xs, p["w_gate"], p["w_up"], p["w_down"])
    return dest3, h1, route, yb


def kernel(x, norm1_g, w_in, q_norm_g, k_norm_g, out_norm_a_g, out_norm_b_g, w_out, norm2_g,
           router_group_w, router_group_b, router_expert_w, router_expert_b, w_gate, w_up, w_down,
           final_norm_g):
    batch, seq, dm = x.shape
    assert dm == A_WIDTH + B_WIDTH and norm1_g.shape[0] == 1
    assert seq % max(ATTN_B_POS, PROJ_ROWS, ATTN_A_K) == 0
    assert seq // max(DILATIONS) >= ATTN_B_ROWS + 2 * WINDOW_RADIUS
    layer = dict(norm1_g=norm1_g[0], w_in=w_in[0], q_norm_g=q_norm_g[0], k_norm_g=k_norm_g[0],
                 out_norm_a_g=out_norm_a_g[0], out_norm_b_g=out_norm_b_g[0], w_out=w_out[0],
                 norm2_g=norm2_g[0], router_group_w=router_group_w[0],
                 router_group_b=router_group_b[0], router_expert_w=router_expert_w[0],
                 router_expert_b=router_expert_b[0], w_gate=w_gate[0], w_up=w_up[0],
                 w_down=w_down[0])
    dest3, h1, route, yb = _layer(x.reshape(batch * seq, dm), batch, seq, layer)
    out = _final_call(dest3, h1, route, final_norm_g[None, :], yb)
    return out.reshape(batch, seq, dm)
```

```python
import functools

import jax
import jax.numpy as jnp
from jax import lax
from jax.experimental import pallas as pl
from jax.experimental.pallas import tpu as pltpu

F32 = jnp.float32
BF16 = jnp.bfloat16

HEAD_DIM = 64
A_HEADS = 8
A_KV_HEADS = 2
A_GROUP = A_HEADS // A_KV_HEADS
B_HEADS = 8
A_WIDTH = A_HEADS * HEAD_DIM
A_KV_WIDTH = A_KV_HEADS * HEAD_DIM
B_WIDTH = B_HEADS * HEAD_DIM
ROPE_THETA = 10000.0
GRID_W = 64
DILATIONS = (1, 4, 16)
WINDOW_RADIUS = 64
N_GROUPS = 4
EXPERTS_PER_GROUP = 8
N_EXPERTS = N_GROUPS * EXPERTS_PER_GROUP
MOE_BLOCK = 512
EPS = 1e-6
NEG_INF = -1e30
LOG2_E = 1.4426950408889634
SHIFT_MARGIN = 1.02
MAX_FIXED_SHIFT = 50.0

LANES = 128
SUBLANES = 8
MXU_TILE = 256
HEADS_PER_SLAB = LANES // HEAD_DIM
B_SLABS = B_WIDTH // LANES
VT_ROWS = HEAD_DIM + 16
VMEM_LIMIT_BYTES = 56 * 1024 * 1024

PROJ_ROWS = 512
ATTN_A_Q = 128
ATTN_A_K = 512
ATTN_B_ROWS = 128
ATTN_B_POS = 2048
ATTN_B_UNROLL = 8

_QA = (0, A_WIDTH)
_KA = (_QA[1], _QA[1] + 2 * A_KV_WIDTH)
_VA = (_KA[1], _KA[1] + A_KV_WIDTH)
_QB = (_VA[1], _VA[1] + B_WIDTH)
_KB = (_QB[1], _QB[1] + B_WIDTH)
_VB = (_KB[1], _KB[1] + B_WIDTH)
EXT_WIDTH = _VB[1]
QK_A_WIDTH = _KA[1]


def _params(*semantics):
    return pltpu.CompilerParams(dimension_semantics=semantics,
                                vmem_limit_bytes=VMEM_LIMIT_BYTES)


def _rms(xf, g):
    return xf * lax.rsqrt(jnp.mean(xf * xf, axis=-1, keepdims=True) + EPS) * g


def _swap_half_heads(x):
    lane = lax.broadcasted_iota(jnp.int32, x.shape, 1)
    first = (lane % HEAD_DIM) < (HEAD_DIM // 2)
    return jnp.where(first, pltpu.roll(x, LANES - HEAD_DIM // 2, 1),
                     pltpu.roll(x, HEAD_DIM // 2, 1))


def _rotary(x, cos, sin_signed):
    return x * cos + _swap_half_heads(x) * sin_signed


def _proj_kernel(x_ref, g1_ref, w_ref, ones_ref, gqk_ref, cosa_ref, sina_ref, cosb_ref, sinb_ref,
                 qat_ref, ka_ref, vat_ref, qb_ref, *rest):
    npat = len(DILATIONS)
    kd_refs, vd_refs = rest[:npat], rest[npat:2 * npat]
    kst_ref, vst_ref = rest[2 * npat:]
    tm = x_ref.shape[0]
    u = _rms(x_ref[...], g1_ref[...]).astype(BF16)

    def proj(cols):
        return jnp.dot(u, w_ref[:, cols[0]:cols[1]], preferred_element_type=F32)

    a = proj((0, QK_A_WIDTH))
    sq = (a * a).astype(BF16)
    ss = jnp.concatenate(
        [jnp.dot(sq[:, j:j + MXU_TILE], ones_ref[...], preferred_element_type=F32)
         for j in range(0, QK_A_WIDTH, MXU_TILE)], axis=1)
    an = a * lax.rsqrt(ss * (1.0 / HEAD_DIM) + EPS) * gqk_ref[...]
    cosa, sina = cosa_ref[...], sina_ref[...]
    for c in range(QK_A_WIDTH // LANES):
        r = _rotary(an[:, c * LANES:(c + 1) * LANES], cosa, sina)
        if c < A_WIDTH // LANES:
            qat_ref[0, c * LANES:(c + 1) * LANES, :] = r.T.astype(BF16)
        else:
            c2 = c - A_WIDTH // LANES
            lane = lax.broadcasted_iota(jnp.int32, r.shape, 1)
            r = jnp.where(lane < HEAD_DIM, r, jnp.where(lane == HEAD_DIM, 1.0, 0.0))
            ka_ref[:, c2 * LANES:(c2 + 1) * LANES] = r.astype(BF16)
    vat = proj(_VA).T
    row = lax.broadcasted_iota(jnp.int32, (VT_ROWS - HEAD_DIM, tm), 0)
    ones_row = jnp.where(row == 0, 1.0, 0.0).astype(BF16)
    for kv in range(A_KV_HEADS):
        vat_ref[0, kv, 0:HEAD_DIM, :] = vat[kv * HEAD_DIM:(kv + 1) * HEAD_DIM].astype(BF16)
        vat_ref[0, kv, HEAD_DIM:VT_ROWS, :] = ones_row

    cosb, sinb = cosb_ref[...], sinb_ref[...]
    t = proj(_QB)
    for c in range(B_SLABS):
        qb_ref[c] = _rotary(t[:, c * LANES:(c + 1) * LANES], cosb, sinb)

    def store_regrouped(val, st_ref, outs, c):
        st_ref[c] = val
        for d, out_ref in zip(DILATIONS, outs):
            for off in range(d):
                rows = val if d == 1 else st_ref[c, pl.ds(off, tm // d, stride=d), :]
                out_ref[c, 0, off] = rows.astype(BF16)

    t = proj(_KB)
    for c in range(B_SLABS):
        store_regrouped(_rotary(t[:, c * LANES:(c + 1) * LANES], cosb, sinb), kst_ref, kd_refs, c)
    t = proj(_VB)
    for c in range(B_SLABS):
        store_regrouped(t[:, c * LANES:(c + 1) * LANES], vst_ref, vd_refs, c)


def _proj_call(x2, g1, w_ext, ones_bd, gqk, cosa, sina, cosb, sinb, seq):
    n, dm = x2.shape
    tm = PROJ_ROWS
    pos_blocks = seq // tm
    const = lambda i: (0, 0)
    row = lambda i: (i, 0)
    pos = lambda i: (i % pos_blocks, 0)
    slab = lambda i: (0, i, 0)
    batch = n // seq
    regrouped_specs = [
        pl.BlockSpec((B_SLABS, 1, d, tm // d, LANES),
                     lambda i: (0, i // pos_blocks, 0, i % pos_blocks, 0))
        for d in DILATIONS]
    regrouped_shapes = [jax.ShapeDtypeStruct((B_SLABS, batch, d, seq // d, LANES), BF16)
                        for d in DILATIONS]
    return pl.pallas_call(
        _proj_kernel,
        grid=(n // tm,),
        in_specs=[
            pl.BlockSpec((tm, dm), row),
            pl.BlockSpec((1, dm), const),
            pl.BlockSpec((dm, EXT_WIDTH), const),
            pl.BlockSpec((MXU_TILE, MXU_TILE), const),
            pl.BlockSpec((1, QK_A_WIDTH), const),
            pl.BlockSpec((tm, LANES), pos),
            pl.BlockSpec((tm, LANES), pos),
            pl.BlockSpec((tm, LANES), pos),
            pl.BlockSpec((tm, LANES), pos),
        ],
        out_specs=[
            pl.BlockSpec((1, A_WIDTH, tm), lambda i: (i // pos_blocks, 0, i % pos_blocks)),
            pl.BlockSpec((tm, 2 * A_KV_WIDTH), row),
            pl.BlockSpec((1, A_KV_HEADS, VT_ROWS, tm),
                         lambda i: (i // pos_blocks, 0, 0, i % pos_blocks)),
            pl.BlockSpec((B_SLABS, tm, LANES), slab),
        ] + regrouped_specs + regrouped_specs,
        out_shape=[
            jax.ShapeDtypeStruct((batch, A_WIDTH, seq), BF16),
            jax.ShapeDtypeStruct((n, 2 * A_KV_WIDTH), BF16),
            jax.ShapeDtypeStruct((batch, A_KV_HEADS, VT_ROWS, seq), BF16),
            jax.ShapeDtypeStruct((B_SLABS, n, LANES), F32),
        ] + regrouped_shapes + regrouped_shapes,
        scratch_shapes=[pltpu.VMEM((B_SLABS, tm, LANES), F32), pltpu.VMEM((B_SLABS, tm, LANES), F32)],
        compiler_params=_params("parallel"),
        name="proj",
    )(x2, g1, w_ext, ones_bd, gqk, cosa, sina, cosb, sinb)


def _attn_a_kernel(qt_ref, k_ref, vt_ref, o_ref, w_ref, m_ref, acc_ref, s_ref, *, tq, tk, nk):
    nq = A_GROUP * tq
    w_ref[HEAD_DIM:, :] = jnp.zeros((LANES - HEAD_DIM, nq), BF16)
    for g in range(A_GROUP):
        w_ref[0:HEAD_DIM, g * tq:(g + 1) * tq] = qt_ref[0, g * HEAD_DIM:(g + 1) * HEAD_DIM, :]
    m_ref[...] = jnp.full(m_ref.shape, NEG_INF, F32)
    acc_ref[...] = jnp.zeros(acc_ref.shape, F32)

    def chunk_start(c):
        return c * tk if isinstance(c, int) else pl.multiple_of(c * tk, tk)

    def scores(c):
        k = k_ref[0, pl.ds(chunk_start(c), tk), :]
        return jnp.dot(k, w_ref[...], preferred_element_type=F32)

    def accumulate(s, c):
        vt = vt_ref[0, 0, :, pl.ds(chunk_start(c), tk)]
        s3 = s.reshape(tk // 8, 8, nq)
        m_prev = m_ref[...]
        m_new = jnp.maximum(m_prev, jnp.max(jnp.max(s3, axis=0), axis=0, keepdims=True))
        p = jnp.exp2(s3 - m_new[None]).reshape(tk, nq).astype(BF16)
        acc = acc_ref[...].reshape(VT_ROWS // 8, 8, nq) * jnp.exp2(m_prev - m_new)[None]
        acc_ref[...] = acc.reshape(VT_ROWS, nq) + jnp.dot(vt, p, preferred_element_type=F32)
        m_ref[...] = m_new

    def pair(c0, last):
        s1 = scores(c0 + 1)
        accumulate(s_ref[...], c0)
        if not last:
            s_ref[...] = scores(c0 + 2)
        accumulate(s1, c0 + 1)

    s_ref[...] = scores(0)

    def body(i, carry):
        pair(2 * i, False)
        return carry

    lax.fori_loop(0, nk // 2 - 1, body, 0)
    pair(nk - 2, True)
    _attn_a_store(acc_ref[...], o_ref, tq)


def _attn_a_shift_kernel(shift_ref, qt_ref, k_ref, vt_ref, o_ref, w_ref, *, tq, tk, nk):
    nq = A_GROUP * tq
    q_all = jnp.concatenate(
        [qt_ref[0, g * HEAD_DIM:(g + 1) * HEAD_DIM, :] for g in range(A_GROUP)], axis=1)
    row = lax.broadcasted_iota(jnp.int32, (16, nq), 0)
    shift_rows = jnp.where(row == 0, -shift_ref[0], 0.0).astype(BF16)
    w_ref[...] = jnp.concatenate(
        [q_all, shift_rows, jnp.zeros((LANES - HEAD_DIM - 16, nq), BF16)], axis=0)

    def scores(c):
        return jnp.dot(k_ref[0, c * tk:(c + 1) * tk, :], w_ref[...], preferred_element_type=F32)

    acc = jnp.zeros((HEAD_DIM, nq), F32)
    den = jnp.zeros((SUBLANES, nq), F32)
    s = scores(0)
    for c in range(nk):
        s_next = scores(c + 1) if c + 1 < nk else None
        p = jnp.exp2(s)
        den = den + jnp.sum(p.reshape(tk // SUBLANES, SUBLANES, nq), axis=0)
        acc = acc + jnp.dot(vt_ref[0, 0, 0:HEAD_DIM, c * tk:(c + 1) * tk], p.astype(BF16),
                            preferred_element_type=F32)
        s = s_next
    den = jnp.sum(den, axis=0, keepdims=True)
    _attn_a_store(jnp.concatenate([acc, jnp.broadcast_to(den, (SUBLANES, nq))], axis=0), o_ref, tq)


def _attn_a_store(acc, o_ref, tq):
    ot = acc[0:HEAD_DIM] / acc[HEAD_DIM:HEAD_DIM + 1]
    for j in range(A_GROUP // 2):
        pair = jnp.concatenate([ot[:, (2 * j) * tq:(2 * j + 1) * tq],
                                ot[:, (2 * j + 1) * tq:(2 * j + 2) * tq]], axis=0)
        o_ref[0, :, j * LANES:(j + 1) * LANES] = pair.T.astype(o_ref.dtype)


def _attn_a_call(qat, ka, vat, shift):
    b, _, s = qat.shape
    tq, tk = ATTN_A_Q, ATTN_A_K
    group_w = A_GROUP * HEAD_DIM
    nq = A_GROUP * tq
    in_specs = [
        pl.BlockSpec((1, group_w, tq), lambda bi, kv, qi: (bi, kv, qi)),
        pl.BlockSpec((1, s, LANES), lambda bi, kv, qi: (bi, 0, kv)),
        pl.BlockSpec((1, 1, VT_ROWS, s), lambda bi, kv, qi: (bi, kv, 0, 0)),
    ]
    common = dict(
        grid=(b, A_KV_HEADS, s // tq),
        out_specs=pl.BlockSpec((1, tq, group_w), lambda bi, kv, qi: (bi, qi, kv)),
        out_shape=jax.ShapeDtypeStruct((b, s, A_WIDTH), BF16),
        compiler_params=_params("parallel", "parallel", "parallel"),
    )

    def fixed_shift():
        return pl.pallas_call(
            functools.partial(_attn_a_shift_kernel, tq=tq, tk=tk, nk=s // tk),
            in_specs=[pl.BlockSpec(memory_space=pltpu.SMEM)] + in_specs,
            scratch_shapes=[pltpu.VMEM((LANES, nq), BF16)],
            name="attn_a_shift", **common)(shift, qat, ka, vat)

    def running_max():
        return pl.pallas_call(
            functools.partial(_attn_a_kernel, tq=tq, tk=tk, nk=s // tk),
            in_specs=in_specs,
            scratch_shapes=[
                pltpu.VMEM((LANES, nq), BF16),
                pltpu.VMEM((8, nq), F32),
                pltpu.VMEM((VT_ROWS, nq), F32),
                pltpu.VMEM((tk, nq), F32),
            ],
            name="attn_a", **common)(qat, ka, vat)

    return lax.cond(shift[0] <= MAX_FIXED_SHIFT, fixed_shift, running_max)


def _attn_b_kernel(q_ref, *rest, seq):
    npat = len(DILATIONS)
    kd_refs, vd_refs = rest[:npat], rest[npat:2 * npat]
    o_ref, bias_ref, op_ref, mp_ref, dp_ref = rest[2 * npat:]
    rb = ATTN_B_ROWS
    kw = rb + 2 * WINDOW_RADIUS
    step = pl.program_id(2)
    lane = lax.broadcasted_iota(jnp.int32, (rb, LANES), 1)
    upper = lane >= HEAD_DIM
    blocks = ATTN_B_POS // rb

    @pl.when(step == 0)
    def _():
        rel = (lax.broadcasted_iota(jnp.int32, (rb, kw), 0)
               - lax.broadcasted_iota(jnp.int32, (rb, kw), 1))
        for case in range(3):
            ok = jnp.abs(rel + WINDOW_RADIUS * case) <= WINDOW_RADIUS
            bias_ref[case] = jnp.where(ok, 0.0, NEG_INF)

    def block_scores(pi, dil, u):
        sub_len = seq // dil
        rt, off = u // dil, u % dil
        j0 = step * (ATTN_B_POS // dil) + rt * rb
        ws = jnp.clip(j0 - WINDOW_RADIUS, 0, sub_len - kw)
        qrows = pl.ds(rt * rb * dil + off, rb, stride=dil)
        krows = (off, pl.ds(pl.multiple_of(ws, WINDOW_RADIUS), kw))
        q = q_ref[0, 0, qrows, :].astype(BF16)
        zero = jnp.zeros_like(q)
        qs = jnp.concatenate([jnp.where(upper, zero, q), jnp.where(upper, q, zero)], axis=0)
        bias = bias_ref[(j0 - ws) // WINDOW_RADIUS]
        s = lax.dot_general(qs, kd_refs[pi][0, 0, krows[0], krows[1], :], (((1,), (1,)), ((), ())),
                            preferred_element_type=F32)
        return s + jnp.concatenate([bias, bias], axis=0), qrows, krows

    def block_output(pi, s, qrows, krows):
        m = jnp.max(s, axis=-1, keepdims=True)
        p = jnp.exp2(s - m)
        den = jnp.broadcast_to(jnp.sum(p, axis=-1, keepdims=True), (HEADS_PER_SLAB * rb, LANES))
        m = jnp.broadcast_to(m, (HEADS_PER_SLAB * rb, LANES))
        o = jnp.dot(p.astype(BF16), vd_refs[pi][0, 0, krows[0], krows[1], :],
                    preferred_element_type=F32)
        op_ref[pi, qrows, :] = jnp.where(upper, o[rb:], o[:rb])
        mp_ref[pi, qrows, :] = jnp.where(upper, m[rb:], m[:rb])
        dp_ref[pi, qrows, :] = jnp.where(upper, den[rb:], den[:rb])

    for pi, dil in enumerate(DILATIONS):
        def trip(i, carry, pi=pi, dil=dil):
            scored = [block_scores(pi, dil, i * ATTN_B_UNROLL + t) for t in range(ATTN_B_UNROLL)]
            for s, qrows, krows in scored:
                block_output(pi, s, qrows, krows)
            return carry
        lax.fori_loop(0, blocks // ATTN_B_UNROLL, trip, 0)

    mmax = jnp.maximum(jnp.maximum(mp_ref[0], mp_ref[1]), mp_ref[2])
    num = jnp.zeros(mmax.shape, F32)
    den = jnp.zeros(mmax.shape, F32)
    for pi in range(len(DILATIONS)):
        e = jnp.exp2(mp_ref[pi] - mmax)
        num = num + e * op_ref[pi]
        den = den + e * dp_ref[pi]
    o_ref[0, 0] = (num / den).astype(o_ref.dtype)


def _attn_b_call(q4, kds, vds):
    nslab, b, s, _ = q4.shape
    tile = pl.BlockSpec((1, 1, ATTN_B_POS, LANES), lambda sl, bi, t: (sl, bi, t, 0))
    full = [pl.BlockSpec((1, 1, d, s // d, LANES), lambda sl, bi, t: (sl, bi, 0, 0, 0))
            for d in DILATIONS]
    npat = len(DILATIONS)
    kw = ATTN_B_ROWS + 2 * WINDOW_RADIUS
    return pl.pallas_call(
        functools.partial(_attn_b_kernel, seq=s),
        grid=(nslab, b, s // ATTN_B_POS),
        in_specs=[tile] + full + full,
        out_specs=tile,
        out_shape=jax.ShapeDtypeStruct((nslab, b, s, LANES), BF16),
        scratch_shapes=[pltpu.VMEM((3, ATTN_B_ROWS, kw), F32),
                        pltpu.VMEM((npat, ATTN_B_POS, LANES), F32),
                        pltpu.VMEM((npat, ATTN_B_POS, LANES), F32),
                        pltpu.VMEM((npat, ATTN_B_POS, LANES), F32)],
        compiler_params=_params("parallel", "parallel", "arbitrary"),
        name="attn_b",
    )(q4, *kds, *vds)


def _out_route_kernel(x_ref, oa_ref, ob_ref, ga_ref, gb_ref, wout_ref, g2_ref, wr_ref, rb_ref,
                      tri_ref, h_ref, route_ref, counts_ref, base_ref):
    tm = x_ref.shape[0]

    @pl.when(pl.program_id(0) == 0)
    def _():
        base_ref[...] = jnp.zeros(base_ref.shape, F32)

    oa = _rms(oa_ref[...].astype(F32), ga_ref[...])
    ob = jnp.concatenate([ob_ref[c].astype(F32) for c in range(B_SLABS)], axis=-1)
    ob = _rms(ob, gb_ref[...])
    cat = jnp.concatenate([oa, ob], axis=-1).astype(BF16)
    h = x_ref[...] + jnp.dot(cat, wout_ref[...], preferred_element_type=F32)
    h_ref[...] = h

    xt = _rms(h, g2_ref[...])
    xh = xt.astype(BF16)
    xl = (xt - xh.astype(F32)).astype(BF16)
    both = jnp.dot(xh, wr_ref[...], preferred_element_type=F32)
    logits = (both[:, :LANES] + both[:, LANES:]
              + jnp.dot(xl, wr_ref[:, :LANES], preferred_element_type=F32)) + rb_ref[...]

    lane = lax.broadcasted_iota(jnp.int32, (tm, LANES), 1).astype(F32)
    none = float(LANES)

    def first_argmax(vals):
        top = jnp.max(vals, axis=-1, keepdims=True)
        idx = jnp.min(jnp.where(vals == top, lane, none), axis=-1, keepdims=True)
        return top, idx

    gl = jnp.where(lane < N_GROUPS, logits, -jnp.inf)
    gmax, gidx = first_argmax(gl)
    gprob = 1.0 / jnp.sum(jnp.exp(gl - gmax), axis=-1, keepdims=True)
    lo = N_GROUPS + EXPERTS_PER_GROUP * gidx
    el = jnp.where((lane >= lo) & (lane < lo + EXPERTS_PER_GROUP), logits, -jnp.inf)
    v1, i1 = first_argmax(el)
    v2, i2 = first_argmax(jnp.where(lane == i1, -jnp.inf, el))
    t = jnp.exp(v2 - v1)
    gate1 = gprob / (1.0 + t)
    gate2 = gprob * t / (1.0 + t)
    x1 = i1 - N_GROUPS
    x2 = i2 - N_GROUPS

    hot1 = lane == x1
    hot2 = lane == x2
    hot = jnp.where(hot1 | hot2, 1.0, 0.0)
    before = base_ref[...] + jnp.dot(tri_ref[...], hot.astype(BF16), preferred_element_type=F32)
    r1 = jnp.sum(jnp.where(hot1, before, 0.0), axis=-1, keepdims=True)
    r2 = jnp.sum(jnp.where(hot2, before, 0.0), axis=-1, keepdims=True)
    base = base_ref[...] + jnp.sum(hot, axis=0, keepdims=True)
    base_ref[...] = base
    counts_ref[...] = jnp.broadcast_to(base, counts_ref.shape)

    packed = jnp.zeros((tm, LANES), F32)
    for i, col in enumerate((x1, x2, gate1, gate2, r1, r2)):
        packed = jnp.where(lane == i, col, packed)
    route_ref[...] = packed


def _out_route_call(x2, oa, ob, ga, gb, wout, g2, wr2, rbias, tri):
    n, dm = x2.shape
    tm = PROJ_ROWS
    const = lambda i: (0, 0)
    row = lambda i: (i, 0)
    slab = pl.BlockSpec((B_SLABS, tm, LANES), lambda i: (0, i, 0))
    return pl.pallas_call(
        _out_route_kernel,
        grid=(n // tm,),
        in_specs=[
            pl.BlockSpec((tm, dm), row),
            pl.BlockSpec((tm, A_WIDTH), row),
            slab,
            pl.BlockSpec((1, A_WIDTH), const),
            pl.BlockSpec((1, B_WIDTH), const),
            pl.BlockSpec((A_WIDTH + B_WIDTH, dm), const),
            pl.BlockSpec((1, dm), const),
            pl.BlockSpec((dm, 2 * LANES), const),
            pl.BlockSpec((1, LANES), const),
            pl.BlockSpec((tm, tm), const),
        ],
        out_specs=[
            pl.BlockSpec((tm, dm), row),
            pl.BlockSpec((tm, LANES), row),
            pl.BlockSpec((8, LANES), const),
        ],
        out_shape=[
            jax.ShapeDtypeStruct((n, dm), F32),
            jax.ShapeDtypeStruct((n, LANES), F32),
            jax.ShapeDtypeStruct((8, LANES), F32),
        ],
        scratch_shapes=[pltpu.VMEM((1, LANES), F32)],
        compiler_params=_params("arbitrary"),
        name="out_route",
    )(x2, oa, ob, ga, gb, wout, g2, wr2, rbias, tri)


def _tile_row(ref, group, sub):
    return ref.at[group, pl.ds(sub, 1)]


def _hbm_row(ref, row):
    return ref.at[pl.ds(row, 1)]


def _block_copy(src_ref, dst_ref, dst_row, sem):
    return pltpu.make_async_copy(src_ref, dst_ref.at[pl.ds(dst_row, MOE_BLOCK)], sem)


def _push_kernel(tail_ref, dest_ref, h_ref, g2_ref, xs_ref, xt_ref, zero_ref, sem, row_sem):
    tm = h_ref.shape[0]

    @pl.when(pl.program_id(0) == 0)
    def _():
        zero_ref[...] = jnp.zeros(zero_ref.shape, F32)
        for e in range(tail_ref.shape[0]):
            @pl.when(tail_ref[e] >= 0)
            def _():
                _block_copy(zero_ref, xs_ref, pl.multiple_of(tail_ref[e], MOE_BLOCK), sem).start()
        for e in range(tail_ref.shape[0]):
            @pl.when(tail_ref[e] >= 0)
            def _():
                _block_copy(zero_ref, xs_ref, pl.multiple_of(tail_ref[e], MOE_BLOCK), sem).wait()

    i = pl.program_id(0)
    slot = i % 2
    xt_ref[slot] = _rms(h_ref[...], g2_ref[...]).reshape(xt_ref.shape[1:])

    def start(g, carry):
        for sub in range(SUBLANES):
            for c in range(2):
                dst = dest_ref[0, 0, 2 * SUBLANES * g + 2 * sub + c]
                pltpu.make_async_copy(_tile_row(xt_ref.at[slot], g, sub), _hbm_row(xs_ref, dst),
                                      row_sem.at[slot]).start(priority=c)
        return carry

    lax.fori_loop(0, tm // SUBLANES, start, 0)

    def drain(slot):
        def wait(g, carry):
            for _ in range(2 * SUBLANES):
                pltpu.make_async_copy(_tile_row(xt_ref.at[slot], 0, 0), _hbm_row(xs_ref, 0),
                                      row_sem.at[slot]).wait()
            return carry
        lax.fori_loop(0, tm // SUBLANES, wait, 0)

    @pl.when(i > 0)
    def _():
        drain(1 - slot)

    @pl.when(i == pl.num_programs(0) - 1)
    def _():
        drain(slot)


def _push_call(tail_rows, dest3, h, g2, rows):
    n, dm = h.shape
    tm = PROJ_ROWS
    return pl.pallas_call(
        _push_kernel,
        grid=(n // tm,),
        in_specs=[
            pl.BlockSpec(memory_space=pltpu.SMEM),
            pl.BlockSpec((1, 1, 2 * tm), lambda i: (i, 0, 0), memory_space=pltpu.SMEM),
            pl.BlockSpec((tm, dm), lambda i: (i, 0)),
            pl.BlockSpec((1, dm), lambda i: (0, 0)),
        ],
        out_specs=pl.BlockSpec(memory_space=pl.ANY),
        out_shape=jax.ShapeDtypeStruct((rows, dm), F32),
        scratch_shapes=[pltpu.VMEM((2, tm // SUBLANES, SUBLANES, dm), F32),
                        pltpu.VMEM((MOE_BLOCK, dm), F32),
                        pltpu.SemaphoreType.DMA(()),
                        pltpu.SemaphoreType.DMA((2,))],
        compiler_params=_params("arbitrary"),
        name="push",
    )(tail_rows, dest3, h, g2)


def _moe_kernel(be_ref, nv_ref, xs_ref, wg_ref, wu_ref, wd_ref, y_ref, wgb_ref, wub_ref, wdb_ref):
    i = pl.program_id(0)
    used = i < nv_ref[0]

    @pl.when(jnp.logical_not(used))
    def _():
        y_ref[...] = jnp.zeros(y_ref.shape, F32)

    @pl.when(used & ((i == 0) | (be_ref[i] != be_ref[jnp.maximum(i - 1, 0)])))
    def _():
        wgb_ref[...] = wg_ref[0].astype(BF16)
        wub_ref[...] = wu_ref[0].astype(BF16)
        wdb_ref[...] = wd_ref[0].astype(BF16)

    @pl.when(used)
    def _():
        xb = xs_ref[...].astype(BF16)
        a = jnp.dot(xb, wgb_ref[...], preferred_element_type=F32)
        u = jnp.dot(xb, wub_ref[...], preferred_element_type=F32)
        hdn = (a / (1.0 + jnp.exp(-a))) * u
        y_ref[...] = jnp.dot(hdn.astype(BF16), wdb_ref[...], preferred_element_type=F32)


def _moe_call(block_e, n_valid, xs, wg, wu, wd):
    rows, dm = xs.shape
    de = wg.shape[-1]
    blk = lambda i, be, nv: (jnp.minimum(i, nv[0] - 1), 0)
    wsel = lambda i, be, nv: (be[jnp.minimum(i, nv[0] - 1)], 0, 0)
    grid_spec = pltpu.PrefetchScalarGridSpec(
        num_scalar_prefetch=2,
        grid=(rows // MOE_BLOCK,),
        in_specs=[
            pl.BlockSpec((MOE_BLOCK, dm), blk),
            pl.BlockSpec((1, dm, de), wsel),
            pl.BlockSpec((1, dm, de), wsel),
            pl.BlockSpec((1, de, dm), wsel),
        ],
        out_specs=pl.BlockSpec((MOE_BLOCK, dm), lambda i, be, nv: (i, 0)),
        scratch_shapes=[pltpu.VMEM((dm, de), BF16), pltpu.VMEM((dm, de), BF16),
                        pltpu.VMEM((de, dm), BF16)],
    )
    return pl.pallas_call(
        _moe_kernel,
        grid_spec=grid_spec,
        out_shape=jax.ShapeDtypeStruct((rows, dm), F32),
        compiler_params=_params("arbitrary"),
        name="moe",
    )(block_e, n_valid, xs, wg, wu, wd)


def _final_kernel(dest_ref, next_dest_ref, h_ref, route_ref, gf_ref, yb_ref, out_ref, y_ref, sem):
    tm = h_ref.shape[0]
    i = pl.program_id(0)
    slot = i % 2

    def gather(idx_ref, slot):
        def start(g, carry):
            for sub in range(SUBLANES):
                for c in range(2):
                    src = idx_ref[0, 0, 2 * SUBLANES * g + 2 * sub + c]
                    pltpu.make_async_copy(_hbm_row(yb_ref, src), _tile_row(y_ref.at[slot, c], g, sub),
                                          sem.at[slot]).start(priority=c)
            return carry
        lax.fori_loop(0, tm // SUBLANES, start, 0)

    @pl.when(i == 0)
    def _():
        gather(dest_ref, 0)

    @pl.when(i + 1 < pl.num_programs(0))
    def _():
        gather(next_dest_ref, 1 - slot)

    def wait(g, carry):
        for _ in range(2 * SUBLANES):
            pltpu.make_async_copy(_hbm_row(yb_ref, 0), _tile_row(y_ref.at[slot, 0], 0, 0),
                                  sem.at[slot]).wait()
        return carry

    lax.fori_loop(0, tm // SUBLANES, wait, 0)
    route = route_ref[...]
    dm = h_ref.shape[1]
    y = (route[:, 2:3] * y_ref[slot, 0].reshape(tm, dm)
         + route[:, 3:4] * y_ref[slot, 1].reshape(tm, dm))
    out_ref[...] = _rms(h_ref[...] + y, gf_ref[...])


def _final_call(dest3, h, route, gf, yb):
    n, dm = h.shape
    tm = PROJ_ROWS
    last = n // tm - 1
    return pl.pallas_call(
        _final_kernel,
        grid=(n // tm,),
        in_specs=[
            pl.BlockSpec((1, 1, 2 * tm), lambda i: (i, 0, 0), memory_space=pltpu.SMEM),
            pl.BlockSpec((1, 1, 2 * tm), lambda i: (jnp.minimum(i + 1, last), 0, 0),
                         memory_space=pltpu.SMEM),
            pl.BlockSpec((tm, dm), lambda i: (i, 0)),
            pl.BlockSpec((tm, LANES), lambda i: (i, 0)),
            pl.BlockSpec((1, dm), lambda i: (0, 0)),
            pl.BlockSpec(memory_space=pl.ANY),
        ],
        out_specs=pl.BlockSpec((tm, dm), lambda i: (i, 0)),
        out_shape=jax.ShapeDtypeStruct((n, dm), F32),
        scratch_shapes=[pltpu.VMEM((2, 2, tm // SUBLANES, SUBLANES, dm), F32),
                        pltpu.SemaphoreType.DMA((2,))],
        compiler_params=_params("arbitrary"),
        name="final",
    )(dest3, dest3, h, route, gf, yb)


def _rope_inv_freq(dim):
    return 1.0 / (ROPE_THETA ** (jnp.arange(0, dim, 2, dtype=F32) / dim))


def _rotary_tables(angles):
    cos = jnp.tile(jnp.cos(angles), (1, LANES // angles.shape[1]))
    sin = jnp.sin(angles)
    sin = jnp.tile(jnp.concatenate([-sin, sin], axis=-1), (1, HEADS_PER_SLAB))
    return cos, sin


def _axial_angles(seq):
    rows = seq // GRID_W
    row = jnp.repeat(jnp.arange(rows, dtype=F32), GRID_W)
    col = jnp.tile(jnp.arange(GRID_W, dtype=F32), rows)
    f = _rope_inv_freq(HEAD_DIM // 2)
    return jnp.concatenate([row[:, None] * f, col[:, None] * f], axis=-1)


def _linear_angles(seq):
    return jnp.arange(seq, dtype=F32)[:, None] * _rope_inv_freq(HEAD_DIM)


def _extended_w_in(w):
    scale = HEAD_DIM ** -0.5
    o = 0
    qa = w[:, o:o + A_WIDTH]; o += A_WIDTH
    ka = w[:, o:o + A_KV_WIDTH]; o += A_KV_WIDTH
    va = w[:, o:o + A_KV_WIDTH]; o += A_KV_WIDTH
    qb = w[:, o:o + B_WIDTH]; o += B_WIDTH
    kb = w[:, o:o + B_WIDTH]; o += B_WIDTH
    vb = w[:, o:o + B_WIDTH]

    def dup(t):
        heads = [t[:, h * HEAD_DIM:(h + 1) * HEAD_DIM] for h in range(A_KV_HEADS)]
        return jnp.concatenate([p for h in heads for p in (h, h)], axis=1)

    return jnp.concatenate([qa, dup(ka), va, qb * (scale * LOG2_E), kb, vb], axis=1).astype(BF16)


def _layer(h2, batch, seq, p):
    n, dm = h2.shape
    scale = HEAD_DIM ** -0.5
    gqk = jnp.concatenate([jnp.tile(p["q_norm_g"] * (scale * LOG2_E), A_HEADS),
                           jnp.tile(p["k_norm_g"], 2 * A_KV_HEADS)])[None, :]
    gi = jnp.arange(MXU_TILE) // HEAD_DIM
    ones_bd = (gi[:, None] == gi[None, :]).astype(BF16)
    cosa, sina = _rotary_tables(_axial_angles(seq))
    cosb, sinb = _rotary_tables(_linear_angles(seq))

    qat, ka, vat, qb, *kvd = _proj_call(h2, p["norm1_g"][None, :], _extended_w_in(p["w_in"]),
                                        ones_bd, gqk, cosa, sina, cosb, sinb, seq)
    kds, vds = kvd[:len(DILATIONS)], kvd[len(DILATIONS):]

    shift = (HEAD_DIM * scale * LOG2_E * SHIFT_MARGIN
             * jnp.max(jnp.abs(p["q_norm_g"])) * jnp.max(jnp.abs(p["k_norm_g"])))
    shift = shift.astype(BF16).astype(F32)[None]
    oa = _attn_a_call(qat, ka.reshape(batch, seq, 2 * A_KV_WIDTH), vat, shift).reshape(n, A_WIDTH)

    ob = _attn_b_call(qb.reshape(B_SLABS, batch, seq, LANES), kds, vds).reshape(B_SLABS, n, LANES)

    wr = jnp.concatenate([p["router_group_w"], p["router_expert_w"]], axis=1)
    wr = jnp.pad(wr, ((0, 0), (0, LANES - wr.shape[1])))
    whi = wr.astype(BF16)
    wlo = (wr - whi.astype(F32)).astype(BF16)
    rbias = jnp.concatenate([p["router_group_b"], p["router_expert_b"]])
    rbias = jnp.pad(rbias, (0, LANES - rbias.shape[0]))[None, :]
    ti = jnp.arange(PROJ_ROWS)
    tri = (ti[:, None] > ti[None, :]).astype(BF16)

    h1, route, counts = _out_route_call(
        h2, oa, ob, p["out_norm_a_g"][None, :], p["out_norm_b_g"][None, :], p["w_out"].astype(BF16),
        p["norm2_g"][None, :], jnp.concatenate([whi, wlo], axis=1), rbias, tri)

    counts = counts[0, :N_EXPERTS].astype(jnp.int32)
    padded = (counts + MOE_BLOCK - 1) // MOE_BLOCK * MOE_BLOCK
    pend = jnp.cumsum(padded)
    pstart = pend - padded
    expert = route[:, 0:2].astype(jnp.int32)
    hot = expert[:, :, None] == jnp.arange(N_EXPERTS, dtype=jnp.int32)
    dest = jnp.sum(jnp.where(hot, pstart, 0), axis=-1) + route[:, 4:6].astype(jnp.int32)
    dest3 = dest.reshape(n // PROJ_ROWS, 1, 2 * PROJ_ROWS)
    n_blocks = -(-(2 * n) // MOE_BLOCK) + N_EXPERTS
    block_row = jnp.arange(n_blocks, dtype=jnp.int32) * MOE_BLOCK
    block_e = jnp.minimum(jnp.sum((block_row[:, None] >= pend[None, :]).astype(jnp.int32), axis=1),
                          N_EXPERTS - 1)
    n_valid = (pend[-1:] // MOE_BLOCK).astype(jnp.int32)
    unused = pend[-1] + jnp.arange(N_EXPERTS, dtype=jnp.int32) * MOE_BLOCK
    tail_rows = jnp.concatenate([jnp.where(padded > 0, pend - MOE_BLOCK, -1),
                                 jnp.where(unused < n_blocks * MOE_BLOCK, unused, -1)])
    tail_rows = tail_rows.astype(jnp.int32)

    xs = _push_call(tail_rows, dest3, h1, p["norm2_g"][None, :], n_blocks * MOE_BLOCK)
    yb = _moe_call(block_e, n_valid, xs, p["w_gate"], p["w_up"], p["w_down"])
    return dest3, h1, route, yb


def kernel(x, norm1_g, w_in, q_norm_g, k_norm_g, out_norm_a_g, out_norm_b_g, w_out, norm2_g,
           router_group_w, router_group_b, router_expert_w, router_expert_b, w_gate, w_up, w_down,
           final_norm_g):
    batch, seq, dm = x.shape
    assert dm == A_WIDTH + B_WIDTH and norm1_g.shape[0] == 1
    assert seq % max(ATTN_B_POS, PROJ_ROWS, ATTN_A_K) == 0
    assert seq // max(DILATIONS) >= ATTN_B_ROWS + 2 * WINDOW_RADIUS
    layer = dict(norm1_g=norm1_g[0], w_in=w_in[0], q_norm_g=q_norm_g[0], k_norm_g=k_norm_g[0],
                 out_norm_a_g=out_norm_a_g[0], out_norm_b_g=out_norm_b_g[0], w_out=w_out[0],
                 norm2_g=norm2_g[0], router_group_w=router_group_w[0],
                 router_group_b=router_group_b[0], router_expert_w=router_expert_w[0],
                 router_expert_b=router_expert_b[0], w_gate=w_gate[0], w_up=w_up[0],
                 w_down=w_down[0])
    dest3, h1, route, yb = _layer(x.reshape(batch * seq, dm), batch, seq, layer)
    out = _final_call(dest3, h1, route, final_norm_g[None, :], yb)
    return out.reshape(batch, seq, dm)
```

```python
import functools

import jax
import jax.numpy as jnp
from jax import lax
from jax.experimental import pallas as pl
from jax.experimental.pallas import tpu as pltpu

F32 = jnp.float32
BF16 = jnp.bfloat16

HEAD_DIM = 64
A_HEADS = 8
A_KV_HEADS = 2
A_GROUP = A_HEADS // A_KV_HEADS
B_HEADS = 8
A_WIDTH = A_HEADS * HEAD_DIM
A_KV_WIDTH = A_KV_HEADS * HEAD_DIM
B_WIDTH = B_HEADS * HEAD_DIM
ROPE_THETA = 10000.0
GRID_W = 64
DILATIONS = (1, 4, 16)
WINDOW_RADIUS = 64
N_GROUPS = 4
EXPERTS_PER_GROUP = 8
N_EXPERTS = N_GROUPS * EXPERTS_PER_GROUP
MOE_BLOCK = 512
EPS = 1e-6
NEG_INF = -1e30
LOG2_E = 1.4426950408889634
SHIFT_MARGIN = 1.02
MAX_FIXED_SHIFT = 50.0

LANES = 128
SUBLANES = 8
MXU_TILE = 256
HEADS_PER_SLAB = LANES // HEAD_DIM
B_SLABS = B_WIDTH // LANES
VT_ROWS = HEAD_DIM + 16
VMEM_LIMIT_BYTES = 56 * 1024 * 1024

PROJ_ROWS = 512
ATTN_A_Q = 128
ATTN_A_K = 512
ATTN_B_ROWS = 128
ATTN_B_POS = 2048
ATTN_B_UNROLL = 8
FINAL_UNROLL = 16

_QA = (0, A_WIDTH)
_KA = (_QA[1], _QA[1] + 2 * A_KV_WIDTH)
_VA = (_KA[1], _KA[1] + A_KV_WIDTH)
_QB = (_VA[1], _VA[1] + B_WIDTH)
_KB = (_QB[1], _QB[1] + B_WIDTH)
_VB = (_KB[1], _KB[1] + B_WIDTH)
EXT_WIDTH = _VB[1]
QK_A_WIDTH = _KA[1]


def _params(*semantics):
    return pltpu.CompilerParams(dimension_semantics=semantics,
                                vmem_limit_bytes=VMEM_LIMIT_BYTES)


def _rms(xf, g):
    return xf * lax.rsqrt(jnp.mean(xf * xf, axis=-1, keepdims=True) + EPS) * g


def _swap_half_heads(x):
    lane = lax.broadcasted_iota(jnp.int32, x.shape, 1)
    first = (lane % HEAD_DIM) < (HEAD_DIM // 2)
    return jnp.where(first, pltpu.roll(x, LANES - HEAD_DIM // 2, 1),
                     pltpu.roll(x, HEAD_DIM // 2, 1))


def _rotary(x, cos, sin_signed):
    return x * cos + _swap_half_heads(x) * sin_signed


def _proj_kernel(x_ref, g1_ref, w_ref, ones_ref, gqk_ref, cosa_ref, sina_ref, cosb_ref, sinb_ref,
                 qat_ref, ka_ref, vat_ref, qb_ref, *rest):
    npat = len(DILATIONS)
    kd_refs, vd_refs = rest[:npat], rest[npat:2 * npat]
    kst_ref, vst_ref = rest[2 * npat:]
    tm = x_ref.shape[0]
    u = _rms(x_ref[...], g1_ref[...]).astype(BF16)

    def proj(cols):
        return jnp.dot(u, w_ref[:, cols[0]:cols[1]], preferred_element_type=F32)

    a = proj((0, QK_A_WIDTH))
    sq = (a * a).astype(BF16)
    ss = jnp.concatenate(
        [jnp.dot(sq[:, j:j + MXU_TILE], ones_ref[...], preferred_element_type=F32)
         for j in range(0, QK_A_WIDTH, MXU_TILE)], axis=1)
    an = a * lax.rsqrt(ss * (1.0 / HEAD_DIM) + EPS) * gqk_ref[...]
    cosa, sina = cosa_ref[...], sina_ref[...]
    for c in range(QK_A_WIDTH // LANES):
        r = _rotary(an[:, c * LANES:(c + 1) * LANES], cosa, sina)
        if c < A_WIDTH // LANES:
            qat_ref[0, c * LANES:(c + 1) * LANES, :] = r.T.astype(BF16)
        else:
            c2 = c - A_WIDTH // LANES
            lane = lax.broadcasted_iota(jnp.int32, r.shape, 1)
            r = jnp.where(lane < HEAD_DIM, r, jnp.where(lane == HEAD_DIM, 1.0, 0.0))
            ka_ref[:, c2 * LANES:(c2 + 1) * LANES] = r.astype(BF16)
    vat = proj(_VA).T
    row = lax.broadcasted_iota(jnp.int32, (VT_ROWS - HEAD_DIM, tm), 0)
    ones_row = jnp.where(row == 0, 1.0, 0.0).astype(BF16)
    for kv in range(A_KV_HEADS):
        vat_ref[0, kv, 0:HEAD_DIM, :] = vat[kv * HEAD_DIM:(kv + 1) * HEAD_DIM].astype(BF16)
        vat_ref[0, kv, HEAD_DIM:VT_ROWS, :] = ones_row

    cosb, sinb = cosb_ref[...], sinb_ref[...]
    t = proj(_QB)
    for c in range(B_SLABS):
        qb_ref[c] = _rotary(t[:, c * LANES:(c + 1) * LANES], cosb, sinb)

    def store_regrouped(val, st_ref, outs, c):
        st_ref[c] = val
        for d, out_ref in zip(DILATIONS, outs):
            for off in range(d):
                rows = val if d == 1 else st_ref[c, pl.ds(off, tm // d, stride=d), :]
                out_ref[c, 0, off] = rows.astype(BF16)

    t = proj(_KB)
    for c in range(B_SLABS):
        store_regrouped(_rotary(t[:, c * LANES:(c + 1) * LANES], cosb, sinb), kst_ref, kd_refs, c)
    t = proj(_VB)
    for c in range(B_SLABS):
        store_regrouped(t[:, c * LANES:(c + 1) * LANES], vst_ref, vd_refs, c)


def _proj_call(x2, g1, w_ext, ones_bd, gqk, cosa, sina, cosb, sinb, seq):
    n, dm = x2.shape
    tm = PROJ_ROWS
    pos_blocks = seq // tm
    const = lambda i: (0, 0)
    row = lambda i: (i, 0)
    pos = lambda i: (i % pos_blocks, 0)
    slab = lambda i: (0, i, 0)
    batch = n // seq
    regrouped_specs = [
        pl.BlockSpec((B_SLABS, 1, d, tm // d, LANES),
                     lambda i: (0, i // pos_blocks, 0, i % pos_blocks, 0))
        for d in DILATIONS]
    regrouped_shapes = [jax.ShapeDtypeStruct((B_SLABS, batch, d, seq // d, LANES), BF16)
                        for d in DILATIONS]
    return pl.pallas_call(
        _proj_kernel,
        grid=(n // tm,),
        in_specs=[
            pl.BlockSpec((tm, dm), row),
            pl.BlockSpec((1, dm), const),
            pl.BlockSpec((dm, EXT_WIDTH), const),
            pl.BlockSpec((MXU_TILE, MXU_TILE), const),
            pl.BlockSpec((1, QK_A_WIDTH), const),
            pl.BlockSpec((tm, LANES), pos),
            pl.BlockSpec((tm, LANES), pos),
            pl.BlockSpec((tm, LANES), pos),
            pl.BlockSpec((tm, LANES), pos),
        ],
        out_specs=[
            pl.BlockSpec((1, A_WIDTH, tm), lambda i: (i // pos_blocks, 0, i % pos_blocks)),
            pl.BlockSpec((tm, 2 * A_KV_WIDTH), row),
            pl.BlockSpec((1, A_KV_HEADS, VT_ROWS, tm),
                         lambda i: (i // pos_blocks, 0, 0, i % pos_blocks)),
            pl.BlockSpec((B_SLABS, tm, LANES), slab),
        ] + regrouped_specs + regrouped_specs,
        out_shape=[
            jax.ShapeDtypeStruct((batch, A_WIDTH, seq), BF16),
            jax.ShapeDtypeStruct((n, 2 * A_KV_WIDTH), BF16),
            jax.ShapeDtypeStruct((batch, A_KV_HEADS, VT_ROWS, seq), BF16),
            jax.ShapeDtypeStruct((B_SLABS, n, LANES), F32),
        ] + regrouped_shapes + regrouped_shapes,
        scratch_shapes=[pltpu.VMEM((B_SLABS, tm, LANES), F32), pltpu.VMEM((B_SLABS, tm, LANES), F32)],
        compiler_params=_params("parallel"),
        name="proj",
    )(x2, g1, w_ext, ones_bd, gqk, cosa, sina, cosb, sinb)


def _attn_a_kernel(qt_ref, k_ref, vt_ref, o_ref, w_ref, m_ref, acc_ref, s_ref, *, tq, tk, nk):
    nq = A_GROUP * tq
    w_ref[HEAD_DIM:, :] = jnp.zeros((LANES - HEAD_DIM, nq), BF16)
    for g in range(A_GROUP):
        w_ref[0:HEAD_DIM, g * tq:(g + 1) * tq] = qt_ref[0, g * HEAD_DIM:(g + 1) * HEAD_DIM, :]
    m_ref[...] = jnp.full(m_ref.shape, NEG_INF, F32)
    acc_ref[...] = jnp.zeros(acc_ref.shape, F32)

    def chunk_start(c):
        return c * tk if isinstance(c, int) else pl.multiple_of(c * tk, tk)

    def scores(c):
        k = k_ref[0, pl.ds(chunk_start(c), tk), :]
        return jnp.dot(k, w_ref[...], preferred_element_type=F32)

    def accumulate(s, c):
        vt = vt_ref[0, 0, :, pl.ds(chunk_start(c), tk)]
        s3 = s.reshape(tk // 8, 8, nq)
        m_prev = m_ref[...]
        m_new = jnp.maximum(m_prev, jnp.max(jnp.max(s3, axis=0), axis=0, keepdims=True))
        p = jnp.exp2(s3 - m_new[None]).reshape(tk, nq).astype(BF16)
        acc = acc_ref[...].reshape(VT_ROWS // 8, 8, nq) * jnp.exp2(m_prev - m_new)[None]
        acc_ref[...] = acc.reshape(VT_ROWS, nq) + jnp.dot(vt, p, preferred_element_type=F32)
        m_ref[...] = m_new

    def pair(c0, last):
        s1 = scores(c0 + 1)
        accumulate(s_ref[...], c0)
        if not last:
            s_ref[...] = scores(c0 + 2)
        accumulate(s1, c0 + 1)

    s_ref[...] = scores(0)

    def body(i, carry):
        pair(2 * i, False)
        return carry

    lax.fori_loop(0, nk // 2 - 1, body, 0)
    pair(nk - 2, True)
    _attn_a_store(acc_ref[...], o_ref, tq)


def _attn_a_shift_kernel(shift_ref, qt_ref, k_ref, vt_ref, o_ref, w_ref, *, tq, tk, nk):
    nq = A_GROUP * tq
    q_all = jnp.concatenate(
        [qt_ref[0, g * HEAD_DIM:(g + 1) * HEAD_DIM, :] for g in range(A_GROUP)], axis=1)
    row = lax.broadcasted_iota(jnp.int32, (16, nq), 0)
    shift_rows = jnp.where(row == 0, -shift_ref[0], 0.0).astype(BF16)
    w_ref[...] = jnp.concatenate(
        [q_all, shift_rows, jnp.zeros((LANES - HEAD_DIM - 16, nq), BF16)], axis=0)

    def scores(c):
        return jnp.dot(k_ref[0, c * tk:(c + 1) * tk, :], w_ref[...], preferred_element_type=F32)

    acc = jnp.zeros((HEAD_DIM, nq), F32)
    den = jnp.zeros((SUBLANES, nq), F32)
    s = scores(0)
    for c in range(nk):
        s_next = scores(c + 1) if c + 1 < nk else None
        p = jnp.exp2(s)
        den = den + jnp.sum(p.reshape(tk // SUBLANES, SUBLANES, nq), axis=0)
        acc = acc + jnp.dot(vt_ref[0, 0, 0:HEAD_DIM, c * tk:(c + 1) * tk], p.astype(BF16),
                            preferred_element_type=F32)
        s = s_next
    den = jnp.sum(den, axis=0, keepdims=True)
    _attn_a_store(jnp.concatenate([acc, jnp.broadcast_to(den, (SUBLANES, nq))], axis=0), o_ref, tq)


def _attn_a_store(acc, o_ref, tq):
    ot = acc[0:HEAD_DIM] / acc[HEAD_DIM:HEAD_DIM + 1]
    for j in range(A_GROUP // 2):
        pair = jnp.concatenate([ot[:, (2 * j) * tq:(2 * j + 1) * tq],
                                ot[:, (2 * j + 1) * tq:(2 * j + 2) * tq]], axis=0)
        o_ref[0, :, j * LANES:(j + 1) * LANES] = pair.T.astype(o_ref.dtype)


def _attn_a_call(qat, ka, vat, shift):
    b, _, s = qat.shape
    tq, tk = ATTN_A_Q, ATTN_A_K
    group_w = A_GROUP * HEAD_DIM
    nq = A_GROUP * tq
    in_specs = [
        pl.BlockSpec((1, group_w, tq), lambda bi, kv, qi: (bi, kv, qi)),
        pl.BlockSpec((1, s, LANES), lambda bi, kv, qi: (bi, 0, kv)),
        pl.BlockSpec((1, 1, VT_ROWS, s), lambda bi, kv, qi: (bi, kv, 0, 0)),
    ]
    common = dict(
        grid=(b, A_KV_HEADS, s // tq),
        out_specs=pl.BlockSpec((1, tq, group_w), lambda bi, kv, qi: (bi, qi, kv)),
        out_shape=jax.ShapeDtypeStruct((b, s, A_WIDTH), BF16),
        compiler_params=_params("parallel", "parallel", "parallel"),
    )

    def fixed_shift():
        return pl.pallas_call(
            functools.partial(_attn_a_shift_kernel, tq=tq, tk=tk, nk=s // tk),
            in_specs=[pl.BlockSpec(memory_space=pltpu.SMEM)] + in_specs,
            scratch_shapes=[pltpu.VMEM((LANES, nq), BF16)],
            name="attn_a_shift", **common)(shift, qat, ka, vat)

    def running_max():
        return pl.pallas_call(
            functools.partial(_attn_a_kernel, tq=tq, tk=tk, nk=s // tk),
            in_specs=in_specs,
            scratch_shapes=[
                pltpu.VMEM((LANES, nq), BF16),
                pltpu.VMEM((8, nq), F32),
                pltpu.VMEM((VT_ROWS, nq), F32),
                pltpu.VMEM((tk, nq), F32),
            ],
            name="attn_a", **common)(qat, ka, vat)

    return lax.cond(shift[0] <= MAX_FIXED_SHIFT, fixed_shift, running_max)


def _attn_b_kernel(q_ref, *rest, seq):
    npat = len(DILATIONS)
    kd_refs, vd_refs = rest[:npat], rest[npat:2 * npat]
    o_ref, bias_ref, op_ref, mp_ref, dp_ref = rest[2 * npat:]
    rb = ATTN_B_ROWS
    kw = rb + 2 * WINDOW_RADIUS
    step = pl.program_id(2)
    lane = lax.broadcasted_iota(jnp.int32, (rb, LANES), 1)
    upper = lane >= HEAD_DIM
    blocks = ATTN_B_POS // rb

    @pl.when(step == 0)
    def _():
        rel = (lax.broadcasted_iota(jnp.int32, (rb, kw), 0)
               - lax.broadcasted_iota(jnp.int32, (rb, kw), 1))
        for case in range(3):
            ok = jnp.abs(rel + WINDOW_RADIUS * case) <= WINDOW_RADIUS
            bias_ref[case] = jnp.where(ok, 0.0, NEG_INF)

    def block_scores(pi, dil, u):
        sub_len = seq // dil
        rt, off = u // dil, u % dil
        j0 = step * (ATTN_B_POS // dil) + rt * rb
        ws = jnp.clip(j0 - WINDOW_RADIUS, 0, sub_len - kw)
        qrows = pl.ds(rt * rb * dil + off, rb, stride=dil)
        krows = (off, pl.ds(pl.multiple_of(ws, WINDOW_RADIUS), kw))
        q = q_ref[0, 0, qrows, :].astype(BF16)
        zero = jnp.zeros_like(q)
        qs = jnp.concatenate([jnp.where(upper, zero, q), jnp.where(upper, q, zero)], axis=0)
        bias = bias_ref[(j0 - ws) // WINDOW_RADIUS]
        s = lax.dot_general(qs, kd_refs[pi][0, 0, krows[0], krows[1], :], (((1,), (1,)), ((), ())),
                            preferred_element_type=F32)
        return s + jnp.concatenate([bias, bias], axis=0), qrows, krows

    def block_output(pi, s, qrows, krows):
        m = jnp.max(s, axis=-1, keepdims=True)
        p = jnp.exp2(s - m)
        den = jnp.broadcast_to(jnp.sum(p, axis=-1, keepdims=True), (HEADS_PER_SLAB * rb, LANES))
        m = jnp.broadcast_to(m, (HEADS_PER_SLAB * rb, LANES))
        o = jnp.dot(p.astype(BF16), vd_refs[pi][0, 0, krows[0], krows[1], :],
                    preferred_element_type=F32)
        op_ref[pi, qrows, :] = jnp.where(upper, o[rb:], o[:rb])
        mp_ref[pi, qrows, :] = jnp.where(upper, m[rb:], m[:rb])
        dp_ref[pi, qrows, :] = jnp.where(upper, den[rb:], den[:rb])

    for pi, dil in enumerate(DILATIONS):
        def trip(i, carry, pi=pi, dil=dil):
            scored = [block_scores(pi, dil, i * ATTN_B_UNROLL + t) for t in range(ATTN_B_UNROLL)]
            for s, qrows, krows in scored:
                block_output(pi, s, qrows, krows)
            return carry
        lax.fori_loop(0, blocks // ATTN_B_UNROLL, trip, 0)

    mmax = jnp.maximum(jnp.maximum(mp_ref[0], mp_ref[1]), mp_ref[2])
    num = jnp.zeros(mmax.shape, F32)
    den = jnp.zeros(mmax.shape, F32)
    for pi in range(len(DILATIONS)):
        e = jnp.exp2(mp_ref[pi] - mmax)
        num = num + e * op_ref[pi]
        den = den + e * dp_ref[pi]
    o_ref[0, 0] = (num / den).astype(o_ref.dtype)


def _attn_b_call(q4, kds, vds):
    nslab, b, s, _ = q4.shape
    tile = pl.BlockSpec((1, 1, ATTN_B_POS, LANES), lambda sl, bi, t: (sl, bi, t, 0))
    full = [pl.BlockSpec((1, 1, d, s // d, LANES), lambda sl, bi, t: (sl, bi, 0, 0, 0))
            for d in DILATIONS]
    npat = len(DILATIONS)
    kw = ATTN_B_ROWS + 2 * WINDOW_RADIUS
    return pl.pallas_call(
        functools.partial(_attn_b_kernel, seq=s),
        grid=(nslab, b, s // ATTN_B_POS),
        in_specs=[tile] + full + full,
        out_specs=tile,
        out_shape=jax.ShapeDtypeStruct((nslab, b, s, LANES), BF16),
        scratch_shapes=[pltpu.VMEM((3, ATTN_B_ROWS, kw), F32),
                        pltpu.VMEM((npat, ATTN_B_POS, LANES), F32),
                        pltpu.VMEM((npat, ATTN_B_POS, LANES), F32),
                        pltpu.VMEM((npat, ATTN_B_POS, LANES), F32)],
        compiler_params=_params("parallel", "parallel", "arbitrary"),
        name="attn_b",
    )(q4, *kds, *vds)


def _out_route_kernel(x_ref, oa_ref, ob_ref, ga_ref, gb_ref, wout_ref, g2_ref, wr_ref, rb_ref,
                      tri_ref, h_ref, route_ref, counts_ref, base_ref):
    tm = x_ref.shape[0]

    @pl.when(pl.program_id(0) == 0)
    def _():
        base_ref[...] = jnp.zeros(base_ref.shape, F32)

    oa = _rms(oa_ref[...].astype(F32), ga_ref[...])
    ob = jnp.concatenate([ob_ref[c].astype(F32) for c in range(B_SLABS)], axis=-1)
    ob = _rms(ob, gb_ref[...])
    cat = jnp.concatenate([oa, ob], axis=-1).astype(BF16)
    h = x_ref[...] + jnp.dot(cat, wout_ref[...], preferred_element_type=F32)
    h_ref[...] = h

    xt = _rms(h, g2_ref[...])
    xh = xt.astype(BF16)
    xl = (xt - xh.astype(F32)).astype(BF16)
    both = jnp.dot(xh, wr_ref[...], preferred_element_type=F32)
    logits = (both[:, :LANES] + both[:, LANES:]
              + jnp.dot(xl, wr_ref[:, :LANES], preferred_element_type=F32)) + rb_ref[...]

    lane = lax.broadcasted_iota(jnp.int32, (tm, LANES), 1).astype(F32)
    none = float(LANES)

    def first_argmax(vals):
        top = jnp.max(vals, axis=-1, keepdims=True)
        idx = jnp.min(jnp.where(vals == top, lane, none), axis=-1, keepdims=True)
        return top, idx

    gl = jnp.where(lane < N_GROUPS, logits, -jnp.inf)
    gmax, gidx = first_argmax(gl)
    gprob = 1.0 / jnp.sum(jnp.exp(gl - gmax), axis=-1, keepdims=True)
    lo = N_GROUPS + EXPERTS_PER_GROUP * gidx
    el = jnp.where((lane >= lo) & (lane < lo + EXPERTS_PER_GROUP), logits, -jnp.inf)
    v1, i1 = first_argmax(el)
    v2, i2 = first_argmax(jnp.where(lane == i1, -jnp.inf, el))
    t = jnp.exp(v2 - v1)
    gate1 = gprob / (1.0 + t)
    gate2 = gprob * t / (1.0 + t)
    x1 = i1 - N_GROUPS
    x2 = i2 - N_GROUPS

    hot1 = lane == x1
    hot2 = lane == x2
    hot = jnp.where(hot1 | hot2, 1.0, 0.0)
    before = base_ref[...] + jnp.dot(tri_ref[...], hot.astype(BF16), preferred_element_type=F32)
    r1 = jnp.sum(jnp.where(hot1, before, 0.0), axis=-1, keepdims=True)
    r2 = jnp.sum(jnp.where(hot2, before, 0.0), axis=-1, keepdims=True)
    base = base_ref[...] + jnp.sum(hot, axis=0, keepdims=True)
    base_ref[...] = base
    counts_ref[...] = jnp.broadcast_to(base, counts_ref.shape)

    packed = jnp.zeros((tm, LANES), F32)
    for i, col in enumerate((x1, x2, gate1, gate2, r1, r2)):
        packed = jnp.where(lane == i, col, packed)
    route_ref[...] = packed


def _out_route_call(x2, oa, ob, ga, gb, wout, g2, wr2, rbias, tri):
    n, dm = x2.shape
    tm = PROJ_ROWS
    const = lambda i: (0, 0)
    row = lambda i: (i, 0)
    slab = pl.BlockSpec((B_SLABS, tm, LANES), lambda i: (0, i, 0))
    return pl.pallas_call(
        _out_route_kernel,
        grid=(n // tm,),
        in_specs=[
            pl.BlockSpec((tm, dm), row),
            pl.BlockSpec((tm, A_WIDTH), row),
            slab,
            pl.BlockSpec((1, A_WIDTH), const),
            pl.BlockSpec((1, B_WIDTH), const),
            pl.BlockSpec((A_WIDTH + B_WIDTH, dm), const),
            pl.BlockSpec((1, dm), const),
            pl.BlockSpec((dm, 2 * LANES), const),
            pl.BlockSpec((1, LANES), const),
            pl.BlockSpec((tm, tm), const),
        ],
        out_specs=[
            pl.BlockSpec((tm, dm), row),
            pl.BlockSpec((tm, LANES), row),
            pl.BlockSpec((8, LANES), const),
        ],
        out_shape=[
            jax.ShapeDtypeStruct((n, dm), F32),
            jax.ShapeDtypeStruct((n, LANES), F32),
            jax.ShapeDtypeStruct((8, LANES), F32),
        ],
        scratch_shapes=[pltpu.VMEM((1, LANES), F32)],
        compiler_params=_params("arbitrary"),
        name="out_route",
    )(x2, oa, ob, ga, gb, wout, g2, wr2, rbias, tri)


def _tile_row(ref, group, sub):
    return ref.at[group, pl.ds(sub, 1)]


def _hbm_row(ref, row):
    return ref.at[pl.ds(row, 1)]


def _block_copy(src_ref, dst_ref, dst_row, sem):
    return pltpu.make_async_copy(src_ref, dst_ref.at[pl.ds(dst_row, MOE_BLOCK)], sem)


def _push_kernel(tail_ref, dest_ref, h_ref, g2_ref, xs_ref, xt_ref, zero_ref, sem, row_sem):
    tm = h_ref.shape[0]

    @pl.when(pl.program_id(0) == 0)
    def _():
        zero_ref[...] = jnp.zeros(zero_ref.shape, F32)
        for e in range(tail_ref.shape[0]):
            @pl.when(tail_ref[e] >= 0)
            def _():
                _block_copy(zero_ref, xs_ref, pl.multiple_of(tail_ref[e], MOE_BLOCK), sem).start()
        for e in range(tail_ref.shape[0]):
            @pl.when(tail_ref[e] >= 0)
            def _():
                _block_copy(zero_ref, xs_ref, pl.multiple_of(tail_ref[e], MOE_BLOCK), sem).wait()

    i = pl.program_id(0)
    slot = i % 2
    xt_ref[slot] = _rms(h_ref[...], g2_ref[...]).reshape(xt_ref.shape[1:])

    def start(g, carry):
        for sub in range(SUBLANES):
            for c in range(2):
                dst = dest_ref[0, 0, 2 * SUBLANES * g + 2 * sub + c]
                pltpu.make_async_copy(_tile_row(xt_ref.at[slot], g, sub), _hbm_row(xs_ref, dst),
                                      row_sem.at[slot]).start(priority=c)
        return carry

    lax.fori_loop(0, tm // SUBLANES, start, 0)

    def drain(slot):
        def wait(g, carry):
            for _ in range(2 * SUBLANES):
                pltpu.make_async_copy(_tile_row(xt_ref.at[slot], 0, 0), _hbm_row(xs_ref, 0),
                                      row_sem.at[slot]).wait()
            return carry
        lax.fori_loop(0, tm // SUBLANES, wait, 0)

    @pl.when(i > 0)
    def _():
        drain(1 - slot)

    @pl.when(i == pl.num_programs(0) - 1)
    def _():
        drain(slot)


def _push_call(tail_rows, dest3, h, g2, rows):
    n, dm = h.shape
    tm = PROJ_ROWS
    return pl.pallas_call(
        _push_kernel,
        grid=(n // tm,),
        in_specs=[
            pl.BlockSpec(memory_space=pltpu.SMEM),
            pl.BlockSpec((1, 1, 2 * tm), lambda i: (i, 0, 0), memory_space=pltpu.SMEM),
            pl.BlockSpec((tm, dm), lambda i: (i, 0)),
            pl.BlockSpec((1, dm), lambda i: (0, 0)),
        ],
        out_specs=pl.BlockSpec(memory_space=pl.ANY),
        out_shape=jax.ShapeDtypeStruct((rows, dm), F32),
        scratch_shapes=[pltpu.VMEM((2, tm // SUBLANES, SUBLANES, dm), F32),
                        pltpu.VMEM((MOE_BLOCK, dm), F32),
                        pltpu.SemaphoreType.DMA(()),
                        pltpu.SemaphoreType.DMA((2,))],
        compiler_params=_params("arbitrary"),
        name="push",
    )(tail_rows, dest3, h, g2)


def _moe_kernel(be_ref, nv_ref, xs_ref, wg_ref, wu_ref, wd_ref, y_ref, wgb_ref, wub_ref, wdb_ref):
    i = pl.program_id(0)
    used = i < nv_ref[0]

    @pl.when(jnp.logical_not(used))
    def _():
        y_ref[...] = jnp.zeros(y_ref.shape, F32)

    @pl.when(used & ((i == 0) | (be_ref[i] != be_ref[jnp.maximum(i - 1, 0)])))
    def _():
        wgb_ref[...] = wg_ref[0].astype(BF16)
        wub_ref[...] = wu_ref[0].astype(BF16)
        wdb_ref[...] = wd_ref[0].astype(BF16)

    @pl.when(used)
    def _():
        xb = xs_ref[...].astype(BF16)
        a = jnp.dot(xb, wgb_ref[...], preferred_element_type=F32)
        u = jnp.dot(xb, wub_ref[...], preferred_element_type=F32)
        hdn = (a / (1.0 + jnp.exp(-a))) * u
        y_ref[...] = jnp.dot(hdn.astype(BF16), wdb_ref[...], preferred_element_type=F32)


def _moe_call(block_e, n_valid, xs, wg, wu, wd):
    rows, dm = xs.shape
    de = wg.shape[-1]
    blk = lambda i, be, nv: (jnp.minimum(i, nv[0] - 1), 0)
    wsel = lambda i, be, nv: (be[jnp.minimum(i, nv[0] - 1)], 0, 0)
    grid_spec = pltpu.PrefetchScalarGridSpec(
        num_scalar_prefetch=2,
        grid=(rows // MOE_BLOCK,),
        in_specs=[
            pl.BlockSpec((MOE_BLOCK, dm), blk),
            pl.BlockSpec((1, dm, de), wsel),
            pl.BlockSpec((1, dm, de), wsel),
            pl.BlockSpec((1, de, dm), wsel),
        ],
        out_specs=pl.BlockSpec((MOE_BLOCK, dm), lambda i, be, nv: (i, 0)),
        scratch_shapes=[pltpu.VMEM((dm, de), BF16), pltpu.VMEM((dm, de), BF16),
                        pltpu.VMEM((de, dm), BF16)],
    )
    return pl.pallas_call(
        _moe_kernel,
        grid_spec=grid_spec,
        out_shape=jax.ShapeDtypeStruct((rows, dm), F32),
        compiler_params=_params("arbitrary"),
        name="moe",
    )(block_e, n_valid, xs, wg, wu, wd)


def _final_kernel(dest_ref, next_dest_ref, h_ref, route_ref, gf_ref, yb_ref, out_ref, y_ref, sem):
    tm = h_ref.shape[0]
    i = pl.program_id(0)
    slot = i % 2

    def issue_group(idx_ref, slot, g):
        for sub in range(SUBLANES):
            for c in range(2):
                src = idx_ref[0, 0, 2 * SUBLANES * g + 2 * sub + c]
                pltpu.make_async_copy(_hbm_row(yb_ref, src), _tile_row(y_ref.at[slot, c], g, sub),
                                      sem.at[slot]).start(priority=c)

    def combine_group(g):
        rows = pl.ds(pl.multiple_of(g * SUBLANES, SUBLANES), SUBLANES)
        route = route_ref[rows, :]
        y = route[:, 2:3] * y_ref[slot, 0, g] + route[:, 3:4] * y_ref[slot, 1, g]
        out_ref[rows, :] = _rms(h_ref[rows, :] + y, gf_ref[...])

    @pl.when(i == 0)
    def _():
        def start(g, carry):
            issue_group(dest_ref, 0, g)
            return carry
        lax.fori_loop(0, tm // SUBLANES, start, 0)

    def wait(g, carry):
        for _ in range(2 * SUBLANES):
            pltpu.make_async_copy(_hbm_row(yb_ref, 0), _tile_row(y_ref.at[slot, 0], 0, 0),
                                  sem.at[slot]).wait()
        return carry

    lax.fori_loop(0, tm // SUBLANES, wait, 0)

    @pl.when(i + 1 < pl.num_programs(0))
    def _():
        def fused(g, carry):
            issue_group(next_dest_ref, 1 - slot, g)
            combine_group(g)
            return carry
        lax.fori_loop(0, tm // SUBLANES, fused, 0, unroll=FINAL_UNROLL)

    @pl.when(i + 1 == pl.num_programs(0))
    def _():
        def last(g, carry):
            combine_group(g)
            return carry
        lax.fori_loop(0, tm // SUBLANES, last, 0, unroll=FINAL_UNROLL)


def _final_call(dest3, h, route, gf, yb):
    n, dm = h.shape
    tm = PROJ_ROWS
    last = n // tm - 1
    return pl.pallas_call(
        _final_kernel,
        grid=(n // tm,),
        in_specs=[
            pl.BlockSpec((1, 1, 2 * tm), lambda i: (i, 0, 0), memory_space=pltpu.SMEM),
            pl.BlockSpec((1, 1, 2 * tm), lambda i: (jnp.minimum(i + 1, last), 0, 0),
                         memory_space=pltpu.SMEM),
            pl.BlockSpec((tm, dm), lambda i: (i, 0)),
            pl.BlockSpec((tm, LANES), lambda i: (i, 0)),
            pl.BlockSpec((1, dm), lambda i: (0, 0)),
            pl.BlockSpec(memory_space=pl.ANY),
        ],
        out_specs=pl.BlockSpec((tm, dm), lambda i: (i, 0)),
        out_shape=jax.ShapeDtypeStruct((n, dm), F32),
        scratch_shapes=[pltpu.VMEM((2, 2, tm // SUBLANES, SUBLANES, dm), F32),
                        pltpu.SemaphoreType.DMA((2,))],
        compiler_params=_params("arbitrary"),
        name="final",
    )(dest3, dest3, h, route, gf, yb)


def _rope_inv_freq(dim):
    return 1.0 / (ROPE_THETA ** (jnp.arange(0, dim, 2, dtype=F32) / dim))


def _rotary_tables(angles):
    cos = jnp.tile(jnp.cos(angles), (1, LANES // angles.shape[1]))
    sin = jnp.sin(angles)
    sin = jnp.tile(jnp.concatenate([-sin, sin], axis=-1), (1, HEADS_PER_SLAB))
    return cos, sin


def _axial_angles(seq):
    rows = seq // GRID_W
    row = jnp.repeat(jnp.arange(rows, dtype=F32), GRID_W)
    col = jnp.tile(jnp.arange(GRID_W, dtype=F32), rows)
    f = _rope_inv_freq(HEAD_DIM // 2)
    return jnp.concatenate([row[:, None] * f, col[:, None] * f], axis=-1)


def _linear_angles(seq):
    return jnp.arange(seq, dtype=F32)[:, None] * _rope_inv_freq(HEAD_DIM)


def _extended_w_in(w):
    scale = HEAD_DIM ** -0.5
    o = 0
    qa = w[:, o:o + A_WIDTH]; o += A_WIDTH
    ka = w[:, o:o + A_KV_WIDTH]; o += A_KV_WIDTH
    va = w[:, o:o + A_KV_WIDTH]; o += A_KV_WIDTH
    qb = w[:, o:o + B_WIDTH]; o += B_WIDTH
    kb = w[:, o:o + B_WIDTH]; o += B_WIDTH
    vb = w[:, o:o + B_WIDTH]

    def dup(t):
        heads = [t[:, h * HEAD_DIM:(h + 1) * HEAD_DIM] for h in range(A_KV_HEADS)]
        return jnp.concatenate([p for h in heads for p in (h, h)], axis=1)

    return jnp.concatenate([qa, dup(ka), va, qb * (scale * LOG2_E), kb, vb], axis=1).astype(BF16)


def _layer(h2, batch, seq, p):
    n, dm = h2.shape
    scale = HEAD_DIM ** -0.5
    gqk = jnp.concatenate([jnp.tile(p["q_norm_g"] * (scale * LOG2_E), A_HEADS),
                           jnp.tile(p["k_norm_g"], 2 * A_KV_HEADS)])[None, :]
    gi = jnp.arange(MXU_TILE) // HEAD_DIM
    ones_bd = (gi[:, None] == gi[None, :]).astype(BF16)
    cosa, sina = _rotary_tables(_axial_angles(seq))
    cosb, sinb = _rotary_tables(_linear_angles(seq))

    qat, ka, vat, qb, *kvd = _proj_call(h2, p["norm1_g"][None, :], _extended_w_in(p["w_in"]),
                                        ones_bd, gqk, cosa, sina, cosb, sinb, seq)
    kds, vds = kvd[:len(DILATIONS)], kvd[len(DILATIONS):]

    shift = (HEAD_DIM * scale * LOG2_E * SHIFT_MARGIN
             * jnp.max(jnp.abs(p["q_norm_g"])) * jnp.max(jnp.abs(p["k_norm_g"])))
    shift = shift.astype(BF16).astype(F32)[None]
    oa = _attn_a_call(qat, ka.reshape(batch, seq, 2 * A_KV_WIDTH), vat, shift).reshape(n, A_WIDTH)

    ob = _attn_b_call(qb.reshape(B_SLABS, batch, seq, LANES), kds, vds).reshape(B_SLABS, n, LANES)

    wr = jnp.concatenate([p["router_group_w"], p["router_expert_w"]], axis=1)
    wr = jnp.pad(wr, ((0, 0), (0, LANES - wr.shape[1])))
    whi = wr.astype(BF16)
    wlo = (wr - whi.astype(F32)).astype(BF16)
    rbias = jnp.concatenate([p["router_group_b"], p["router_expert_b"]])
    rbias = jnp.pad(rbias, (0, LANES - rbias.shape[0]))[None, :]
    ti = jnp.arange(PROJ_ROWS)
    tri = (ti[:, None] > ti[None, :]).astype(BF16)

    h1, route, counts = _out_route_call(
        h2, oa, ob, p["out_norm_a_g"][None, :], p["out_norm_b_g"][None, :], p["w_out"].astype(BF16),
        p["norm2_g"][None, :], jnp.concatenate([whi, wlo], axis=1), rbias, tri)

    counts = counts[0, :N_EXPERTS].astype(jnp.int32)
    padded = (counts + MOE_BLOCK - 1) // MOE_BLOCK * MOE_BLOCK
    pend = jnp.cumsum(padded)
    pstart = pend - padded
    expert = route[:, 0:2].astype(jnp.int32)
    hot = expert[:, :, None] == jnp.arange(N_EXPERTS, dtype=jnp.int32)
    dest = jnp.sum(jnp.where(hot, pstart, 0), axis=-1) + route[:, 4:6].astype(jnp.int32)
    dest3 = dest.reshape(n // PROJ_ROWS, 1, 2 * PROJ_ROWS)
    n_blocks = -(-(2 * n) // MOE_BLOCK) + N_EXPERTS
    block_row = jnp.arange(n_blocks, dtype=jnp.int32) * MOE_BLOCK
    block_e = jnp.minimum(jnp.sum((block_row[:, None] >= pend[None, :]).astype(jnp.int32), axis=1),
                          N_EXPERTS - 1)
    n_valid = (pend[-1:] // MOE_BLOCK).astype(jnp.int32)
    unused = pend[-1] + jnp.arange(N_EXPERTS, dtype=jnp.int32) * MOE_BLOCK
    tail_rows = jnp.concatenate([jnp.where(padded > 0, pend - MOE_BLOCK, -1),
                                 jnp.where(unused < n_blocks * MOE_BLOCK, unused, -1)])
    tail_rows = tail_rows.astype(jnp.int32)

    xs = _push_call(tail_rows, dest3, h1, p["norm2_g"][None, :], n_blocks * MOE_BLOCK)
    yb = _moe_call(block_e, n_valid, xs, p["w_gate"], p["w_up"], p["w_down"])
    return dest3, h1, route, yb


def kernel(x, norm1_g, w_in, q_norm_g, k_norm_g, out_norm_a_g, out_norm_b_g, w_out, norm2_g,
           router_group_w, router_group_b, router_expert_w, router_expert_b, w_gate, w_up, w_down,
           final_norm_g):
    batch, seq, dm = x.shape
    assert dm == A_WIDTH + B_WIDTH and norm1_g.shape[0] == 1
    assert seq % max(ATTN_B_POS, PROJ_ROWS, ATTN_A_K) == 0
    assert seq // max(DILATIONS) >= ATTN_B_ROWS + 2 * WINDOW_RADIUS
    layer = dict(norm1_g=norm1_g[0], w_in=w_in[0], q_norm_g=q_norm_g[0], k_norm_g=k_norm_g[0],
                 out_norm_a_g=out_norm_a_g[0], out_norm_b_g=out_norm_b_g[0], w_out=w_out[0],
                 norm2_g=norm2_g[0], router_group_w=router_group_w[0],
                 router_group_b=router_group_b[0], router_expert_w=router_expert_w[0],
                 router_expert_b=router_expert_b[0], w_gate=w_gate[0], w_up=w_up[0],
                 w_down=w_down[0])
    dest3, h1, route, yb = _layer(x.reshape(batch * seq, dm), batch, seq, layer)
    out = _final_call(dest3, h1, route, final_norm_g[None, :], yb)
    return out.reshape(batch, seq, dm)
```
